```python
import math
import jax, jax.numpy as jnp
from jax import lax
import numpy as np

D_MODEL = 1024
BATCH = 8
SEQ = 8192
DEPTH = 4

CHUNK = 64
QBLK = 128
N_MIXERS = 4
D_MIX = D_MODEL
GROUP = D_MIX // N_MIXERS
HEAD_DIM = 64
N_HEADS = GROUP // HEAD_DIM
ROPE_THETA = 10000.0
EPS = 1e-6
MLA_Q_LORA = GROUP
MLA_KV_LORA = GROUP // 2
MLA_NOPE = HEAD_DIM
MLA_ROPE = HEAD_DIM // 2
MLA_V = HEAD_DIM
MLA_QK = MLA_NOPE + MLA_ROPE
DSA_TOPK_MAX = 256
IDX_HEADS = 4
IDX_DIM = 64
DIFF_DK = HEAD_DIM // 2
DIFF_DV = HEAD_DIM

IN_SIZES = (GROUP, GROUP, GROUP, GROUP,
            MLA_Q_LORA, MLA_KV_LORA, MLA_ROPE, GROUP,
            GROUP, HEAD_DIM, HEAD_DIM, GROUP, IDX_HEADS * IDX_DIM, IDX_DIM, IDX_HEADS,
            GROUP, GROUP, GROUP, GROUP)
IN_COLS = sum(IN_SIZES)

kernel_name = "hybrid_parallel_heads_stickbreak_mla_dsa_diff"


def rms_norm(x, g):
    xf = x.astype(jnp.float32)
    y = xf * lax.rsqrt(jnp.mean(xf * xf, axis=-1, keepdims=True) + EPS)
    return (y * g.astype(jnp.float32)).astype(x.dtype)


def apply_rope(x):
    S, d = x.shape[1], x.shape[-1]
    inv = ROPE_THETA ** (-jnp.arange(0, d, 2, dtype=jnp.float32) / d)
    ang = jnp.arange(S, dtype=jnp.float32)[:, None] * inv[None, :]
    cos = jnp.cos(ang)[None, :, None, :].astype(x.dtype)
    sin = jnp.sin(ang)[None, :, None, :].astype(x.dtype)
    x1, x2 = x[..., : d // 2], x[..., d // 2:]
    return jnp.concatenate([x1 * cos - x2 * sin, x1 * sin + x2 * cos], axis=-1)


def sweep(block_fn, S, q_args, kv_args):
    outs = []
    for i in range(S // QBLK):
        lo, hi = i * QBLK, (i + 1) * QBLK
        outs.append(block_fn(i, *[t[:, lo:hi] for t in q_args], *[t[:, :hi] for t in kv_args]))
    return jnp.concatenate(outs, axis=1)


def chunk_visible(i, Lk):
    tq = i * QBLK + jnp.arange(QBLK)
    chunk_end = (tq // CHUNK + 1) * CHUNK
    return jnp.arange(Lk)[None, :] < chunk_end[:, None]


def reverse_cumsum(x):
    L = x.shape[-1]
    n = L // QBLK
    xb = x.reshape(*x.shape[:-1], n, QBLK)
    r = jnp.arange(QBLK)
    upper = (r[:, None] >= r[None, :]).astype(x.dtype)
    within = jnp.einsum('...nj,js->...ns', xb, upper, precision=lax.Precision.HIGHEST)
    m = jnp.arange(n)
    later = (m[:, None] > m[None, :]).astype(x.dtype)
    after = jnp.einsum('...m,mn->...n', jnp.sum(xb, axis=-1), later, precision=lax.Precision.HIGHEST)
    return (within + after[..., None]).reshape(x.shape)


def stick_breaking(q, k, v):
    S, d = q.shape[1], q.shape[-1]
    scale = d ** -0.5

    def block(i, qb, kk, vv):
        Lk = kk.shape[1]
        tq = i * QBLK + jnp.arange(QBLK)
        strict = jnp.arange(Lk)[None, :] < tq[:, None]
        z = jnp.einsum('bqhd,bshd->bhqs', qb, kk).astype(jnp.float32) * scale
        log_1mb = jnp.where(strict, jax.nn.log_sigmoid(-z), 0.0)
        between = reverse_cumsum(log_1mb) - log_1mb
        a = jnp.where(strict, jnp.exp(jax.nn.log_sigmoid(z) + between), 0.0)
        return jnp.einsum('bhqs,bshd->bqhd', a.astype(vv.dtype), vv)

    return sweep(block, S, (q,), (k, v))


def softmax_attention(q, k, v, scale):
    S = q.shape[1]

    def block(i, qb, kk, vv):
        vis = chunk_visible(i, kk.shape[1])
        s = jnp.einsum('bqhd,bshd->bhqs', qb, kk).astype(jnp.float32) * scale
        p = jax.nn.softmax(jnp.where(vis, s, -jnp.inf), axis=-1)
        return jnp.einsum('bhqs,bshd->bqhd', p.astype(vv.dtype), vv)

    return sweep(block, S, (q,), (k, v))


def diff_attention(q1, q2, k1, k2, v, lam, scale):
    S = q1.shape[1]

    def block(i, q1b, q2b, k1b, k2b, vv):
        vis = chunk_visible(i, vv.shape[1])

        def probs(qb, kk):
            s = jnp.einsum('bqhd,bshd->bhqs', qb, kk).astype(jnp.float32) * scale
            return jax.nn.softmax(jnp.where(vis, s, -jnp.inf), axis=-1)

        w = probs(q1b, k1b) - lam * probs(q2b, k2b)
        return jnp.einsum('bhqs,bshd->bqhd', w.astype(vv.dtype), vv)

    return sweep(block, S, (q1, q2), (k1, k2, v))


def dsa_attention(q, k, v, iq, ik, iw, topk):
    S, d = q.shape[1], q.shape[-1]
    scale = d ** -0.5
    idx_scale = IDX_DIM ** -0.5

    def block(i, qb, iqb, iwb, kk, vv, ikk):
        Lk = kk.shape[1]
        kn = min(topk, Lk)
        adm = chunk_visible(i, Lk)
        logits = jnp.einsum('bqhe,bse->bqhs', iqb, ikk).astype(jnp.float32) * idx_scale
        score = jnp.einsum('bqh,bqhs->bqs', iwb.astype(jnp.float32), jax.nn.relu(logits))
        score = jnp.where(adm[None], score, -jnp.inf)
        top_val, top_idx = lax.top_k(score, kn)
        valid = jnp.isfinite(top_val)
        kg = jax.vmap(lambda a, ii: a[ii])(kk, top_idx)
        vg = jax.vmap(lambda a, ii: a[ii])(vv, top_idx)
        s = jnp.einsum('bqhd,bqkd->bhqk', qb, kg).astype(jnp.float32) * scale
        p = jax.nn.softmax(jnp.where(valid[:, None], s, -jnp.inf), axis=-1)
        return jnp.einsum('bhqk,bqkd->bqhd', p.astype(vg.dtype), vg)

    return sweep(block, S, (q, iq, iw), (k, v, ik))


def hybrid_layer(x, layer, ln_g, w_in, mla_q_norm_g, mla_kv_norm_g, mla_w_uq, mla_w_ukv, mla_q_g, mla_k_g,
                 dsa_q_g, dsa_k_g, diff_q_g, diff_k_g, diff_lq1, diff_lk1, diff_lq2, diff_lk2, diff_subln_g,
                 w_out, topk):
    B, S, _ = x.shape
    h = rms_norm(x, ln_g)
    proj = jnp.einsum('bsd,dn->bsn', h, w_in)
    split_points = np.cumsum(np.array(IN_SIZES))[:-1].tolist()
    (a_q, a_k, a_v, a_g,
     b_cq, b_ckv, b_kr, b_g,
     c_q, c_k, c_v, c_g, c_iq, c_ik, c_iw,
     d_q, d_k, d_v, d_g) = jnp.split(proj, split_points, axis=-1)

    def heads(t, n):
        return t.reshape(B, S, n, -1)

    y_a = stick_breaking(heads(a_q, N_HEADS), heads(a_k, N_HEADS), heads(a_v, N_HEADS))

    cq = rms_norm(b_cq, mla_q_norm_g)
    ckv = rms_norm(b_ckv, mla_kv_norm_g)
    qb = heads(jnp.einsum('bsr,rn->bsn', cq, mla_w_uq), N_HEADS)
    kv = heads(jnp.einsum('bsr,rn->bsn', ckv, mla_w_ukv), N_HEADS)
    q_nope = rms_norm(qb[..., :MLA_NOPE], mla_q_g[:MLA_NOPE])
    q_rope = apply_rope(rms_norm(qb[..., MLA_NOPE:], mla_q_g[MLA_NOPE:]))
    k_nope = rms_norm(kv[..., :MLA_NOPE], mla_k_g[:MLA_NOPE])
    v_b = kv[..., MLA_NOPE:]
    k_rope = apply_rope(rms_norm(b_kr, mla_k_g[MLA_NOPE:])[:, :, None, :])
    k_rope = jnp.broadcast_to(k_rope, (B, S, N_HEADS, MLA_ROPE))
    y_b = softmax_attention(jnp.concatenate([q_nope, q_rope], -1),
                            jnp.concatenate([k_nope, k_rope], -1), v_b, MLA_QK ** -0.5)

    qc = apply_rope(rms_norm(heads(c_q, N_HEADS), dsa_q_g))
    kc = apply_rope(rms_norm(c_k, dsa_k_g)[:, :, None, :])[:, :, 0, :]
    iq = apply_rope(heads(c_iq, IDX_HEADS))
    ik = apply_rope(c_ik[:, :, None, :])[:, :, 0, :]
    iw = c_iw * (IDX_HEADS ** -0.5)
    y_c = dsa_attention(qc, kc, c_v, iq, ik, iw, topk)

    lam_init = 0.8 - 0.6 * math.exp(-0.3 * layer)
    lam = (jnp.exp(jnp.sum(diff_lq1.astype(jnp.float32) * diff_lk1.astype(jnp.float32)))
           - jnp.exp(jnp.sum(diff_lq2.astype(jnp.float32) * diff_lk2.astype(jnp.float32))) + lam_init)
    qd = apply_rope(rms_norm(heads(d_q, 2 * N_HEADS), diff_q_g)).reshape(B, S, N_HEADS, 2, DIFF_DK)
    kd = apply_rope(rms_norm(heads(d_k, 2 * N_HEADS), diff_k_g)).reshape(B, S, N_HEADS, 2, DIFF_DK)
    od = diff_attention(qd[..., 0, :], qd[..., 1, :], kd[..., 0, :], kd[..., 1, :],
                        heads(d_v, N_HEADS), lam, DIFF_DK ** -0.5)
    y_d = rms_norm(od, diff_subln_g) * (1.0 - lam_init)

    y = jnp.concatenate([
        jax.nn.silu(a_g) * y_a.reshape(B, S, GROUP),
        jax.nn.silu(b_g) * y_b.reshape(B, S, GROUP),
        jax.nn.silu(c_g) * y_c.reshape(B, S, GROUP),
        jax.nn.silu(d_g) * y_d.reshape(B, S, GROUP),
    ], axis=-1)
    return x + jnp.einsum('bsn,nd->bsd', y, w_out)


def setup_inputs(seed: int = 0) -> dict:
    key = jax.random.key(seed)
    ks = jax.random.split(key, 20)
    f32 = jnp.float32

    def normal(k, shape, scale):
        return jax.random.normal(k, shape, f32) * scale

    def gain(k, n):
        return 1.0 + 0.02 * jax.random.normal(k, (DEPTH, n), f32)

    return {
        "x": normal(ks[0], (BATCH, SEQ, D_MODEL), 1.0),
        "ln_g": gain(ks[1], D_MODEL),
        "w_in": normal(ks[2], (DEPTH, D_MODEL, IN_COLS), D_MODEL ** -0.5),
        "mla_q_norm_g": gain(ks[3], MLA_Q_LORA),
        "mla_kv_norm_g": gain(ks[4], MLA_KV_LORA),
        "mla_w_uq": normal(ks[5], (DEPTH, MLA_Q_LORA, N_HEADS * MLA_QK), MLA_Q_LORA ** -0.5),
        "mla_w_ukv": normal(ks[6], (DEPTH, MLA_KV_LORA, N_HEADS * (MLA_NOPE + MLA_V)), MLA_KV_LORA ** -0.5),
        "mla_q_g": gain(ks[7], MLA_QK),
        "mla_k_g": gain(ks[8], MLA_QK),
        "dsa_q_g": gain(ks[9], HEAD_DIM),
        "dsa_k_g": gain(ks[10], HEAD_DIM),
        "diff_q_g": gain(ks[11], DIFF_DK),
        "diff_k_g": gain(ks[12], DIFF_DK),
        "diff_lq1": normal(ks[13], (DEPTH, DIFF_DK), 0.1),
        "diff_lk1": normal(ks[14], (DEPTH, DIFF_DK), 0.1),
        "diff_lq2": normal(ks[15], (DEPTH, DIFF_DK), 0.1),
        "diff_lk2": normal(ks[16], (DEPTH, DIFF_DK), 0.1),
        "diff_subln_g": gain(ks[17], DIFF_DV),
        "w_out": normal(ks[18], (DEPTH, D_MIX, D_MODEL), D_MIX ** -0.5),
    }


def reference(x, ln_g, w_in, mla_q_norm_g, mla_kv_norm_g, mla_w_uq, mla_w_ukv, mla_q_g, mla_k_g,
              dsa_q_g, dsa_k_g, diff_q_g, diff_k_g, diff_lq1, diff_lk1, diff_lq2, diff_lk2, diff_subln_g,
              w_out):
    topk = min(DSA_TOPK_MAX, x.shape[1] // 4)
    for l in range(DEPTH):
        x = hybrid_layer(x, l, ln_g[l], w_in[l], mla_q_norm_g[l], mla_kv_norm_g[l], mla_w_uq[l], mla_w_ukv[l],
                         mla_q_g[l], mla_k_g[l], dsa_q_g[l], dsa_k_g[l], diff_q_g[l], diff_k_g[l],
                         diff_lq1[l], diff_lk1[l], diff_lq2[l], diff_lk2[l], diff_subln_g[l], w_out[l], topk)
    return x
```

```python
import functools
import math

import numpy as np
import jax
import jax.numpy as jnp
from jax import lax
from jax.experimental import pallas as pl
from jax.experimental.pallas import tpu as pltpu

F32 = jnp.float32
BF16 = jnp.bfloat16

D_MODEL = 1024
GROUP = 256
HEAD_DIM = 64
N_HEADS = 4
CHUNK = 64
CHUNK_SHIFT = 6
ROPE_THETA = 10000.0
EPS = 1e-6
MLA_NOPE, MLA_ROPE, MLA_QK = 64, 32, 96
DSA_TOPK_MAX = 256
DIFF_DK = 32
LOG2E = 1.4426950408889634

IN_SIZES = (256, 256, 256, 256, 256, 128, 32, 256, 256, 64, 64, 256, 256, 64, 4, 256, 256, 256, 256)
IN_NAMES = ("a_q", "a_k", "a_v", "a_g", "b_cq", "b_ckv", "b_kr", "b_g", "c_q", "c_k", "c_v", "c_g",
            "c_iq", "c_ik", "c_iw", "d_q", "d_k", "d_v", "d_g")

LANES = 128
ROW_TILE = 256
Q_TILE = 256
VMEM_LIMIT = 56 * 1024 * 1024
MASKED = -1e30
KEY_NEG_INF = -2139095041
INT_MIN = -2147483648
EXP_UNDERFLOW = -105.0

_SLABS = (("aq", 256), ("ak", 256), ("av", 256), ("g", 1024), ("bcq", 256), ("bckv", 128),
          ("bkr", 128), ("bkr_rot", 128), ("cq", 256), ("cq_rot", 256), ("ck4", 256), ("ck4_rot", 256),
          ("cv2", 128), ("ciq", 256), ("ciq_rot", 256), ("cik4", 256), ("cik4_rot", 256), ("ciw", 128),
          ("dq", 256), ("dq_rot", 256), ("dk", 256), ("dk_rot", 256), ("dv", 256))
_SLAB = {}
_off = 0
for _n, _w in _SLABS:
    _SLAB[_n] = (_off, _w)
    _off += _w
W_COLS = _off


def _rot_idx(d, n):
    idx = []
    for i in range(n):
        idx += list(range(i * d + d // 2, (i + 1) * d)) + list(range(i * d, i * d + d // 2))
    return np.array(idx, dtype=np.int32)


def _split_cols(w):
    out, off = {}, 0
    for n, s in zip(IN_NAMES, IN_SIZES):
        out[n] = w[..., off:off + s]
        off += s
    return out


def _pad_cols(a, left, total):
    pads = [(0, 0)] * (a.ndim - 1) + [(left, total - left - a.shape[-1])]
    return jnp.pad(a, pads)


def _tile_last(a, n):
    return jnp.concatenate([a] * n, axis=-1)


def _prepare_weights(w_in, mla_w_uq, mla_w_ukv):
    c = _split_cols(w_in)
    r64_4, r64_1, r32_8, r32_1 = _rot_idx(64, 4), _rot_idx(64, 1), _rot_idx(32, 8), _rot_idx(32, 1)
    parts = {
        "aq": c["a_q"], "ak": c["a_k"], "av": c["a_v"],
        "g": jnp.concatenate([c["a_g"], c["b_g"], c["c_g"], c["d_g"]], -1),
        "bcq": c["b_cq"], "bckv": c["b_ckv"],
        "bkr": _pad_cols(c["b_kr"], 64, 128), "bkr_rot": _pad_cols(c["b_kr"][..., r32_1], 64, 128),
        "cq": c["c_q"], "cq_rot": c["c_q"][..., r64_4],
        "ck4": _tile_last(c["c_k"], 4), "ck4_rot": _tile_last(c["c_k"][..., r64_1], 4),
        "cv2": _tile_last(c["c_v"], 2),
        "ciq": c["c_iq"], "ciq_rot": c["c_iq"][..., r64_4],
        "cik4": _tile_last(c["c_ik"], 4), "cik4_rot": _tile_last(c["c_ik"][..., r64_1], 4),
        "ciw": _pad_cols(c["c_iw"], 0, 128),
        "dq": c["d_q"], "dq_rot": c["d_q"][..., r32_8], "dk": c["d_k"], "dk_rot": c["d_k"][..., r32_8],
        "dv": c["d_v"],
    }
    w = jnp.concatenate([parts[n] for n, _ in _SLABS], axis=-1).astype(BF16)

    depth = w_in.shape[0]
    uq = mla_w_uq.reshape(depth, 256, N_HEADS, MLA_QK)
    uq_x = _pad_cols(uq, 0, 128).reshape(depth, 256, 512)
    uq_r = _pad_cols(uq[..., MLA_NOPE:][..., r32_1], 64, 128).reshape(depth, 256, 512)
    wuq = jnp.concatenate([uq_x, uq_r], -1).astype(BF16)
    ukv = mla_w_ukv.reshape(depth, 128, N_HEADS, 128)
    uk = _pad_cols(ukv[..., :MLA_NOPE], 0, 128).reshape(depth, 128, 512)
    uv = ukv[..., MLA_NOPE:].reshape(depth, 128, 256)
    wukv = jnp.concatenate([uk, uv], -1).astype(BF16)
    return w, wuq, wukv


def _prepare_gains(ln_g, mla_q_norm_g, mla_kv_norm_g, mla_q_g, mla_k_g, dsa_q_g, dsa_k_g, diff_q_g, diff_k_g):
    r32, r64 = _rot_idx(32, 1), _rot_idx(64, 1)

    def row(a):
        return _pad_cols(a, 0, 1024)[:, None, :]

    sb = MLA_QK ** -0.5 * LOG2E
    sc = HEAD_DIM ** -0.5 * LOG2E
    sd = DIFF_DK ** -0.5 * LOG2E
    qb = _tile_last(_pad_cols(mla_q_g, 0, 128), 4) * sb
    qb_rot = _tile_last(_pad_cols(mla_q_g[:, MLA_NOPE:][:, r32], 64, 128), 4) * sb
    kb = _tile_last(_pad_cols(mla_k_g[:, :MLA_NOPE], 0, 128), 4)
    kr = _pad_cols(mla_k_g[:, MLA_NOPE:], 64, 128)
    kr_rot = _pad_cols(mla_k_g[:, MLA_NOPE:][:, r32], 64, 128)
    rows = [ln_g, mla_q_norm_g, mla_kv_norm_g, qb, qb_rot, kb, kr, kr_rot,
            _tile_last(dsa_q_g, 4) * sc, _tile_last(dsa_q_g[:, r64], 4) * sc,
            _tile_last(dsa_k_g, 4), _tile_last(dsa_k_g[:, r64], 4),
            _tile_last(diff_q_g, 8) * sd, _tile_last(diff_q_g[:, r32], 8) * sd,
            _tile_last(diff_k_g, 8), _tile_last(diff_k_g[:, r32], 8)]
    return jnp.concatenate([row(r.astype(F32)) for r in rows], axis=1)


def _rope_tables(seq):
    pos = jnp.arange(seq, dtype=F32)[:, None]

    def cs(d):
        inv = ROPE_THETA ** (-jnp.arange(0, d, 2, dtype=F32) / d)
        ang = pos * inv[None, :]
        cos, sin = jnp.cos(ang), jnp.sin(ang)
        return jnp.concatenate([cos, cos], -1), jnp.concatenate([-sin, sin], -1)

    c32, s32 = cs(32)
    c64, s64 = cs(64)
    one, zero = jnp.ones((seq, 64), F32), jnp.zeros((seq, 64), F32)
    z32 = jnp.zeros((seq, 32), F32)
    tb = jnp.stack([jnp.concatenate([one, c32, z32], -1), jnp.concatenate([zero, s32, z32], -1)])
    t64 = jnp.stack([_tile_last(c64, 2), _tile_last(s64, 2)])
    t32 = jnp.stack([_tile_last(c32, 4), _tile_last(s32, 4)])
    return jnp.stack([tb, t64, t32])


def _segment_mean_matrices():
    lane = np.arange(256)

    def seg(ids, sizes):
        same = ids[:, None] == ids[None, :]
        return np.where(same, 1.0 / sizes[None, :], 0.0)

    in128 = lane % 128
    ids_b = (lane // 128) * 3 + np.where(in128 < 64, 0, np.where(in128 < 96, 1, 2))
    sizes_b = np.where(in128 < 64, 64.0, 32.0)
    mats = [seg(ids_b, sizes_b), seg(lane // 64, np.full(256, 64.0)), seg(lane // 32, np.full(256, 32.0))]
    return jnp.asarray(np.stack(mats), dtype=BF16)


def _dot(a, b):
    return jnp.dot(a, b, preferred_element_type=F32)


def _dot_nt(a, b):
    return lax.dot_general(a, b, (((1,), (1,)), ((), ())), preferred_element_type=F32)


def _segment_mean(sq, m):
    hi = sq.astype(BF16)
    lo = (sq - hi.astype(F32)).astype(BF16)
    return _dot(hi, m) + _dot(lo, m)


def _segment_mean_wide(sq, m):
    n = sq.shape[-1] // 256
    return jnp.concatenate([_segment_mean(sq[:, 256 * i:256 * (i + 1)], m) for i in range(n)], axis=-1)


def _proj_kernel(x_ref, w_ref, wuq_ref, wukv_ref, gv_ref, seg_ref, tab_ref,
                 qa_ref, ka_ref, va_ref, gate_ref, qb_ref, kb_ref, vb_ref,
                 qc_ref, kc_ref, vc_ref, iq_ref, ik_ref, iw_ref, qd_ref, kd_ref, vd_ref):
    x = x_ref[...]
    r = lax.rsqrt(jnp.mean(x * x, axis=-1, keepdims=True) + EPS)
    h = (x * r * gv_ref[0:1, :]).astype(BF16)

    def proj(name):
        off, width = _SLAB[name]
        return _dot(h, w_ref[:, off:off + width])

    def gain(row, width):
        return gv_ref[row:row + 1, 0:width]

    seg_b, seg_64, seg_32 = seg_ref[0], seg_ref[1], seg_ref[2]
    cos_b, sin_b = tab_ref[0, 0], tab_ref[0, 1]
    cos_64, sin_64 = _tile_last(tab_ref[1, 0], 2), _tile_last(tab_ref[1, 1], 2)
    cos_32, sin_32 = _tile_last(tab_ref[2, 0], 2), _tile_last(tab_ref[2, 1], 2)

    def normed_rope(xv, xrot, seg, g_row, grot_row, cos, sin):
        width = xv.shape[-1]
        rr = lax.rsqrt(_segment_mean_wide(xv * xv, seg) + EPS)
        return rr * (xv * (gain(g_row, width) * cos) + xrot * (gain(grot_row, width) * sin))

    qa_ref[...] = (proj("aq") * 0.125).astype(BF16)
    ka_ref[...] = proj("ak").astype(BF16)
    va_ref[...] = proj("av").astype(BF16)

    g = proj("g")
    gate_ref[...] = g * jax.nn.sigmoid(g)

    cq = proj("bcq")
    cq = (cq * lax.rsqrt(jnp.mean(cq * cq, axis=-1, keepdims=True) + EPS) * gain(1, 256)).astype(BF16)
    ckv = proj("bckv")
    ckv = (ckv * lax.rsqrt(jnp.mean(ckv * ckv, axis=-1, keepdims=True) + EPS) * gain(2, 128)).astype(BF16)
    q2 = _dot(cq, wuq_ref[...])
    qb_ref[...] = normed_rope(q2[:, :512], q2[:, 512:], seg_b, 3, 4,
                              _tile_last(cos_b, 4), _tile_last(sin_b, 4)).astype(BF16)
    kv2 = _dot(ckv, wukv_ref[...])
    k_nope = kv2[:, :512]
    rk = lax.rsqrt(_segment_mean_wide(k_nope * k_nope, seg_b) + EPS)
    kr, kr_rot = proj("bkr"), proj("bkr_rot")
    rkr = lax.rsqrt(_segment_mean(_tile_last(kr * kr, 2), seg_b)[:, :128] + EPS)
    kr_out = rkr * (kr * (gain(6, 128) * cos_b) + kr_rot * (gain(7, 128) * sin_b))
    kb_ref[...] = (rk * k_nope * gain(5, 512) + _tile_last(kr_out, 4)).astype(BF16)
    vb_ref[...] = kv2[:, 512:].astype(BF16)

    qc_ref[...] = normed_rope(proj("cq"), proj("cq_rot"), seg_64, 8, 9, cos_64, sin_64).astype(BF16)
    kc_ref[...] = normed_rope(proj("ck4"), proj("ck4_rot"), seg_64, 10, 11, cos_64, sin_64).astype(BF16)
    vc_ref[...] = proj("cv2").astype(BF16)
    iq_ref[...] = ((proj("ciq") * cos_64 + proj("ciq_rot") * sin_64) * 0.125).astype(BF16)
    ik_ref[...] = (proj("cik4") * cos_64 + proj("cik4_rot") * sin_64).astype(BF16)
    iw_ref[...] = proj("ciw") * 0.5

    qd_ref[...] = normed_rope(proj("dq"), proj("dq_rot"), seg_32, 12, 13, cos_32, sin_32).astype(BF16)
    kd_ref[...] = normed_rope(proj("dk"), proj("dk_rot"), seg_32, 14, 15, cos_32, sin_32).astype(BF16)
    vd_ref[...] = proj("dv").astype(BF16)


_PROJ_OUT = (("qa", 256, BF16), ("ka", 256, BF16), ("va", 256, BF16), ("gate", 1024, F32),
             ("qb", 512, BF16), ("kb", 512, BF16), ("vb", 256, BF16),
             ("qc", 256, BF16), ("kc", 256, BF16), ("vc", 128, BF16), ("iq", 256, BF16), ("ik", 256, BF16),
             ("iw", 128, F32), ("qd", 256, BF16), ("kd", 256, BF16), ("vd", 256, BF16))


def _project(x2d, w, wuq, wukv, gv, seg, tabs, seq):
    rows = x2d.shape[0]
    tm = ROW_TILE
    n_seq = seq // tm

    def const(shape):
        return pl.BlockSpec(shape, lambda i: (0,) * len(shape))

    in_specs = [pl.BlockSpec((tm, D_MODEL), lambda i: (i, 0)),
                const(w.shape), const(wuq.shape), const(wukv.shape), const(gv.shape), const(seg.shape),
                pl.BlockSpec((3, 2, tm, LANES), lambda i: (0, 0, i % n_seq, 0))]
    out_specs = [pl.BlockSpec((tm, width), lambda i: (i, 0)) for _, width, _ in _PROJ_OUT]
    out_shape = [jax.ShapeDtypeStruct((rows, width), dt) for _, width, dt in _PROJ_OUT]
    outs = pl.pallas_call(
        _proj_kernel, grid=(rows // tm,), in_specs=in_specs, out_specs=out_specs, out_shape=out_shape,
        compiler_params=pltpu.CompilerParams(dimension_semantics=("arbitrary",), vmem_limit_bytes=VMEM_LIMIT),
        name="proj",
    )(x2d, w, wuq, wukv, gv, seg, tabs)
    return {n: o for (n, _, _), o in zip(_PROJ_OUT, outs)}


def _out_kernel(x_ref, gate_ref, ya_ref, yb_ref, yc_ref, yd_ref, w_ref, o_ref):
    y = jnp.concatenate([ya_ref[...], yb_ref[...], yc_ref[...], yd_ref[...]], axis=-1) * gate_ref[...]
    o_ref[...] = x_ref[...] + _dot(y.astype(BF16), w_ref[...])


def _out_project(x2d, gate, ya, yb, yc, yd, w_out):
    rows = x2d.shape[0]
    tm = ROW_TILE
    row_spec = lambda width: pl.BlockSpec((tm, width), lambda i: (i, 0))
    return pl.pallas_call(
        _out_kernel, grid=(rows // tm,),
        in_specs=[row_spec(D_MODEL), row_spec(D_MODEL), row_spec(GROUP), row_spec(GROUP), row_spec(GROUP),
                  row_spec(GROUP), pl.BlockSpec((D_MODEL, D_MODEL), lambda i: (0, 0))],
        out_specs=row_spec(D_MODEL), out_shape=jax.ShapeDtypeStruct((rows, D_MODEL), F32),
        compiler_params=pltpu.CompilerParams(dimension_semantics=("arbitrary",), vmem_limit_bytes=VMEM_LIMIT),
        name="out_proj",
    )(x2d, gate, ya, yb, yc, yd, w_out)


def _tile_iotas(t):
    return lax.broadcasted_iota(jnp.int32, (t, t), 0), lax.broadcasted_iota(jnp.int32, (t, t), 1)


def _chunk_visible(t):
    row, col = _tile_iotas(t)
    return (col >> CHUNK_SHIFT) <= (row >> CHUNK_SHIFT)


def _lane_group_mask(q, shift, group):
    lane = lax.broadcasted_iota(jnp.int32, q.shape, 1)
    return jnp.where((lane >> shift) == group, q, jnp.zeros_like(q))


def _online_softmax_step(s, m, l, acc_ref, idx, v_t):
    m_new = jnp.maximum(m, jnp.max(s, axis=-1, keepdims=True))
    alpha = jnp.exp2(m - m_new)
    p = jnp.exp2(s - m_new)
    l_new = alpha * l + jnp.sum(p, axis=-1, keepdims=True)
    acc_ref[idx] = alpha * acc_ref[idx] + _dot(p.astype(BF16), v_t)
    return m_new, l_new


def _merge_pair(lo, hi):
    lane = lax.broadcasted_iota(jnp.int32, lo.shape, 1)
    return jnp.where(lane < HEAD_DIM, lo, hi)


def _attn_call(kernel, q_arrays, kv_arrays, extra, batch, seq, scratch, name):
    tq = Q_TILE
    nq = seq // tq
    in_specs = ([pl.BlockSpec((tq, a.shape[1]), lambda b, i: (b * nq + i, 0)) for a in q_arrays]
                + [pl.BlockSpec((seq, a.shape[1]), lambda b, i: (b, 0)) for a in kv_arrays]
                + [pl.BlockSpec(a.shape, lambda b, i: (0,) * a.ndim) for a in extra])
    return pl.pallas_call(
        kernel, grid=(batch, nq), in_specs=in_specs,
        out_specs=pl.BlockSpec((tq, GROUP), lambda b, i: (b * nq + i, 0)),
        out_shape=jax.ShapeDtypeStruct((batch * seq, GROUP), F32),
        scratch_shapes=scratch,
        compiler_params=pltpu.CompilerParams(dimension_semantics=("arbitrary", "arbitrary"),
                                             vmem_limit_bytes=VMEM_LIMIT),
        name=name,
    )(*q_arrays, *kv_arrays, *extra)


def _attn_a_kernel(q_ref, k_ref, v_ref, o_ref, acc_ref):
    t = Q_TILE
    i = pl.program_id(1)
    row, col = _tile_iotas(t)
    strict = col < row
    later = (row > col).astype(BF16)
    qs = [_lane_group_mask(q_ref[:, LANES * (h // 2):LANES * (h // 2 + 1)], 6, h % 2) for h in range(N_HEADS)]
    acc_ref[...] = jnp.zeros_like(acc_ref)

    def tile(j, rs, on_diagonal):
        off = pl.multiple_of(j * t, t)
        out = []
        for h in range(N_HEADS):
            lanes = slice(LANES * (h // 2), LANES * (h // 2 + 1))
            z = _dot_nt(qs[h], k_ref[pl.ds(off, t), lanes])
            sp = jnp.maximum(z, 0.0) + jnp.log1p(jnp.exp(-jnp.abs(z)))
            lm = -sp
            if on_diagonal:
                lm = jnp.where(strict, lm, 0.0)
            l1 = lm.astype(BF16)
            r1 = lm - l1.astype(F32)
            l2 = r1.astype(BF16)
            l3 = (r1 - l2.astype(F32)).astype(BF16)
            between = rs[h] + (_dot(l1, later) + _dot(l2, later) + _dot(l3, later))
            a = jnp.exp(z - sp + between)
            if on_diagonal:
                a = jnp.where(strict, a, 0.0)
            acc_ref[h] += _dot(a.astype(BF16), v_ref[pl.ds(off, t), lanes])
            out.append(rs[h] + jnp.sum(lm, axis=-1, keepdims=True))
        return tuple(out)

    def r_max(rs):
        return jnp.max(jnp.maximum(jnp.maximum(rs[0], rs[1]), jnp.maximum(rs[2], rs[3])))

    rs0 = tile(i, tuple(jnp.zeros((t, 1), F32) for _ in range(N_HEADS)), True)

    def cond(state):
        j, _, rmax = state
        return jnp.logical_and(j >= 0, rmax > EXP_UNDERFLOW)

    def body(state):
        j, rs, _ = state
        rs = tile(j, rs, False)
        return j - 1, rs, r_max(rs)

    lax.while_loop(cond, body, (i - 1, rs0, r_max(rs0)))
    for p in range(2):
        o_ref[:, LANES * p:LANES * (p + 1)] = _merge_pair(acc_ref[2 * p], acc_ref[2 * p + 1])


def _attn_b_kernel(q_ref, k_ref, v_ref, o_ref, acc_ref):
    t = Q_TILE
    i = pl.program_id(1)
    visible = _chunk_visible(t)
    qs = [q_ref[:, LANES * h:LANES * (h + 1)] for h in range(N_HEADS)]
    acc_ref[...] = jnp.zeros_like(acc_ref)

    def tile(j, carry, on_diagonal):
        ms, ls = carry
        off = pl.multiple_of(j * t, t)
        new_m, new_l = [], []
        for h in range(N_HEADS):
            s = _dot_nt(qs[h], k_ref[pl.ds(off, t), LANES * h:LANES * (h + 1)])
            if on_diagonal:
                s = jnp.where(visible, s, MASKED)
            v_t = v_ref[pl.ds(off, t), LANES * (h // 2):LANES * (h // 2 + 1)]
            m, l = _online_softmax_step(s, ms[h], ls[h], acc_ref, h, v_t)
            new_m.append(m)
            new_l.append(l)
        return tuple(new_m), tuple(new_l)

    init = (tuple(jnp.full((t, 1), MASKED, F32) for _ in range(N_HEADS)),
            tuple(jnp.zeros((t, 1), F32) for _ in range(N_HEADS)))
    carry = lax.fori_loop(0, i, lambda j, c: tile(j, c, False), init)
    _, ls = tile(i, carry, True)
    for p in range(2):
        o_ref[:, LANES * p:LANES * (p + 1)] = _merge_pair(acc_ref[2 * p] / ls[2 * p],
                                                          acc_ref[2 * p + 1] / ls[2 * p + 1])


def _attn_d_kernel(q_ref, k_ref, v_ref, par_ref, o_ref, acc_ref):
    t = Q_TILE
    i = pl.program_id(1)
    visible = _chunk_visible(t)
    n_sub = 2 * N_HEADS
    qs = [_lane_group_mask(q_ref[:, LANES * (s // 4):LANES * (s // 4 + 1)], 5, s % 4) for s in range(n_sub)]
    acc_ref[...] = jnp.zeros_like(acc_ref)

    def tile(j, carry, on_diagonal):
        ms, ls = carry
        off = pl.multiple_of(j * t, t)
        new_m, new_l = [], []
        for s_i in range(n_sub):
            lanes = slice(LANES * (s_i // 4), LANES * (s_i // 4 + 1))
            s = _dot_nt(qs[s_i], k_ref[pl.ds(off, t), lanes])
            if on_diagonal:
                s = jnp.where(visible, s, MASKED)
            m, l = _online_softmax_step(s, ms[s_i], ls[s_i], acc_ref, s_i, v_ref[pl.ds(off, t), lanes])
            new_m.append(m)
            new_l.append(l)
        return tuple(new_m), tuple(new_l)

    init = (tuple(jnp.full((t, 1), MASKED, F32) for _ in range(n_sub)),
            tuple(jnp.zeros((t, 1), F32) for _ in range(n_sub)))
    carry = lax.fori_loop(0, i, lambda j, c: tile(j, c, False), init)
    _, ls = tile(i, carry, True)

    lam = (jnp.exp(jnp.sum(par_ref[0:1, :] * par_ref[1:2, :], axis=-1, keepdims=True))
           - jnp.exp(jnp.sum(par_ref[2:3, :] * par_ref[3:4, :], axis=-1, keepdims=True)) + par_ref[4:5, 0:1])
    gain = par_ref[5:6, :]
    lane = lax.broadcasted_iota(jnp.int32, (t, LANES), 1)
    ys = []
    for h in range(N_HEADS):
        o = acc_ref[2 * h] / ls[2 * h] - lam * (acc_ref[2 * h + 1] / ls[2 * h + 1])
        own = (lane >> 6) == (h % 2)
        ms_o = jnp.sum(jnp.where(own, o * o, 0.0), axis=-1, keepdims=True) * (1.0 / HEAD_DIM)
        ys.append(o * lax.rsqrt(ms_o + EPS) * gain)
    for p in range(2):
        o_ref[:, LANES * p:LANES * (p + 1)] = _merge_pair(ys[2 * p], ys[2 * p + 1])


def _attn_c_kernel(q_ref, iq_ref, iw_ref, k_ref, v_ref, ik_ref, o_ref, keys_ref, acc_ref, *, topk):
    t = Q_TILE
    i = pl.program_id(1)
    n_kv = i + 1
    visible = _chunk_visible(t)
    kf = float(topk)

    iqs = [_lane_group_mask(iq_ref[...], 6, h) for h in range(N_HEADS)]
    iws = [iw_ref[:, h:h + 1] for h in range(N_HEADS)]

    def score_tile(j, on_diagonal):
        off = pl.multiple_of(j * t, t)
        ik_t = ik_ref[pl.ds(off, t), :]
        score = iws[0] * jnp.maximum(_dot_nt(iqs[0], ik_t), 0.0)
        for h in range(1, N_HEADS):
            score = score + iws[h] * jnp.maximum(_dot_nt(iqs[h], ik_t), 0.0)
        bits = lax.bitcast_convert_type(score, jnp.int32)
        key = jnp.where(bits < 0, bits ^ jnp.int32(0x7FFFFFFF), bits)
        if on_diagonal:
            key = jnp.where(visible, key, jnp.int32(KEY_NEG_INF))
        keys_ref[:, pl.ds(off, t)] = key

    def score_body(j, c):
        score_tile(j, False)
        return c

    lax.fori_loop(0, i, score_body, 0)
    score_tile(i, True)

    def count(pred):
        def body(j, acc):
            off = pl.multiple_of(j * t, t)
            ind = jnp.where(pred(keys_ref[:, pl.ds(off, t)]), 1.0, 0.0)
            return acc + ind[:, :LANES] + ind[:, LANES:]
        acc = lax.fori_loop(0, n_kv, body, jnp.zeros((t, LANES), F32))
        return jnp.sum(acc, axis=-1, keepdims=True)

    def bit_body(b, thr):
        cand = thr + jnp.left_shift(jnp.int32(1), 31 - b)
        return jnp.where(count(lambda kt: kt >= cand) >= kf, cand, thr)

    thr = lax.fori_loop(0, 32, bit_body, jnp.full((t, 1), INT_MIN, jnp.int32))

    n_gt = count(lambda kt: kt > thr)
    n_ge = count(lambda kt: kt >= thr)
    need = kf - n_gt
    surplus = jnp.logical_and(n_ge - n_gt > need, thr > KEY_NEG_INF)
    any_surplus = jnp.max(jnp.where(surplus, 1.0, 0.0)) > 0.0

    @pl.when(any_surplus)
    def _():
        row, col = _tile_iotas(t)
        upto = (row <= col).astype(BF16)

        def body(j, seen):
            off = pl.multiple_of(j * t, t)
            kt = keys_ref[:, pl.ds(off, t)]
            tie = kt == thr
            rank = seen + _dot(jnp.where(tie, 1.0, 0.0).astype(BF16), upto)
            keys_ref[:, pl.ds(off, t)] = jnp.where(jnp.logical_and(tie, rank > need), jnp.int32(INT_MIN), kt)
            return seen + jnp.sum(jnp.where(tie, 1.0, 0.0), axis=-1, keepdims=True)

        lax.fori_loop(0, n_kv, body, jnp.zeros((t, 1), F32))

    thr_sel = jnp.maximum(thr, jnp.int32(KEY_NEG_INF + 1))
    qs = [_lane_group_mask(q_ref[...], 6, h) for h in range(N_HEADS)]
    acc_ref[...] = jnp.zeros_like(acc_ref)

    def attn_body(j, carry):
        ms, ls = carry
        off = pl.multiple_of(j * t, t)
        sel = keys_ref[:, pl.ds(off, t)] >= thr_sel
        k_t = k_ref[pl.ds(off, t), :]
        v_t = v_ref[pl.ds(off, t), :]
        new_m, new_l = [], []
        for h in range(N_HEADS):
            s = jnp.where(sel, _dot_nt(qs[h], k_t), MASKED)
            m, l = _online_softmax_step(s, ms[h], ls[h], acc_ref, h, v_t)
            new_m.append(m)
            new_l.append(l)
        return tuple(new_m), tuple(new_l)

    init = (tuple(jnp.full((t, 1), MASKED, F32) for _ in range(N_HEADS)),
            tuple(jnp.zeros((t, 1), F32) for _ in range(N_HEADS)))
    _, ls = lax.fori_loop(0, n_kv, attn_body, init)
    for p in range(2):
        o_ref[:, LANES * p:LANES * (p + 1)] = _merge_pair(acc_ref[2 * p] / ls[2 * p],
                                                          acc_ref[2 * p + 1] / ls[2 * p + 1])


def _diff_params(lq1, lk1, lq2, lk2, subln_g, lam_init):
    return jnp.concatenate([
        _pad_cols(jnp.stack([lq1, lk1, lq2, lk2]), 0, LANES),
        jnp.full((1, LANES), lam_init, F32),
        _tile_last(subln_g, 2)[None, :] * (1.0 - lam_init),
        jnp.zeros((2, LANES), F32)], axis=0).astype(F32)


def _layer_groups(x2d, batch, seq, w, wuq, wukv, gv, seg, tabs, dpar, topk):
    p = _project(x2d, w, wuq, wukv, gv, seg, tabs, seq)
    head_acc = pltpu.VMEM((N_HEADS, Q_TILE, LANES), F32)
    ya = _attn_call(_attn_a_kernel, [p["qa"]], [p["ka"], p["va"]], [], batch, seq, [head_acc], "attn_a")
    yb = _attn_call(_attn_b_kernel, [p["qb"]], [p["kb"], p["vb"]], [], batch, seq, [head_acc], "attn_b")
    yc = _attn_call(functools.partial(_attn_c_kernel, topk=topk), [p["qc"], p["iq"], p["iw"]],
                    [p["kc"], p["vc"], p["ik"]], [], batch, seq,
                    [pltpu.VMEM((Q_TILE, seq), jnp.int32), head_acc], "attn_c")
    yd = _attn_call(_attn_d_kernel, [p["qd"]], [p["kd"], p["vd"]], [dpar], batch, seq,
                    [pltpu.VMEM((2 * N_HEADS, Q_TILE, LANES), F32)], "attn_d")
    return p, ya, yb, yc, yd


def _layer(x2d, batch, seq, w, wuq, wukv, gv, seg, tabs, dpar, w_out, topk):
    p, ya, yb, yc, yd = _layer_groups(x2d, batch, seq, w, wuq, wukv, gv, seg, tabs, dpar, topk)
    return _out_project(x2d, p["gate"], ya, yb, yc, yd, w_out)


def kernel(x, ln_g, w_in, mla_q_norm_g, mla_kv_norm_g, mla_w_uq, mla_w_ukv, mla_q_g, mla_k_g, dsa_q_g, dsa_k_g,
           diff_q_g, diff_k_g, diff_lq1, diff_lk1, diff_lq2, diff_lk2, diff_subln_g, w_out):
    batch, seq, _ = x.shape
    depth = w_in.shape[0]
    assert seq % Q_TILE == 0 and (batch * seq) % ROW_TILE == 0 and seq % ROW_TILE == 0
    topk = min(DSA_TOPK_MAX, seq // 4)

    w, wuq, wukv = _prepare_weights(w_in, mla_w_uq, mla_w_ukv)
    gv = _prepare_gains(ln_g, mla_q_norm_g, mla_kv_norm_g, mla_q_g, mla_k_g, dsa_q_g, dsa_k_g, diff_q_g, diff_k_g)
    seg = _segment_mean_matrices()
    tabs = _rope_tables(seq)
    w_out_bf = w_out.astype(BF16)

    x2d = x.reshape(batch * seq, D_MODEL)
    for layer in range(depth):
        lam_init = 0.8 - 0.6 * math.exp(-0.3 * layer)
        dpar = _diff_params(diff_lq1[layer], diff_lk1[layer], diff_lq2[layer], diff_lk2[layer],
                            diff_subln_g[layer], lam_init)
        x2d = _layer(x2d, batch, seq, w[layer], wuq[layer], wukv[layer], gv[layer], seg, tabs, dpar,
                     w_out_bf[layer], topk)
    return x2d.reshape(batch, seq, D_MODEL)
```

```python
import functools
import math

import numpy as np
import jax
import jax.numpy as jnp
from jax import lax
from jax.experimental import pallas as pl
from jax.experimental.pallas import tpu as pltpu

F32 = jnp.float32
BF16 = jnp.bfloat16

D_MODEL = 1024
GROUP = 256
HEAD_DIM = 64
N_HEADS = 4
CHUNK_SHIFT = 6
ROPE_THETA = 10000.0
EPS = 1e-6
MLA_NOPE, MLA_ROPE, MLA_QK = 64, 32, 96
DSA_TOPK_MAX = 256
DIFF_DK = 32
LOG2E = 1.4426950408889634

IN_SIZES = (256, 256, 256, 256, 256, 128, 32, 256, 256, 64, 64, 256, 256, 64, 4, 256, 256, 256, 256)
IN_NAMES = ("a_q", "a_k", "a_v", "a_g", "b_cq", "b_ckv", "b_kr", "b_g", "c_q", "c_k", "c_v", "c_g",
            "c_iq", "c_ik", "c_iw", "d_q", "d_k", "d_v", "d_g")

LANES = 128
SUBLANES = 8
TILE = 256
VMEM_LIMIT = 56 * 1024 * 1024
MASKED = -1e30
INT_MIN = -2147483648
KEY_NEG_INF = -2139095040
EXP_UNDERFLOW = -105.0

_SLABS = (("aq", 256), ("ak", 256), ("av", 256), ("g", 1024), ("bcq", 256), ("bckv", 128),
          ("bkr", 128), ("bkr_rot", 128), ("cq", 256), ("cq_rot", 256), ("ck", 128), ("ck_rot", 128),
          ("cv", 128), ("ciq", 256), ("ciq_rot", 256), ("cik", 128), ("cik_rot", 128), ("ciw", 128),
          ("dq", 256), ("dq_rot", 256), ("dk", 256), ("dk_rot", 256), ("dv", 256))
_SLAB = {}
_off = 0
for _n, _w in _SLABS:
    _SLAB[_n] = (_off, _w)
    _off += _w
W_COLS = _off


def _rot_idx(d, n):
    idx = []
    for i in range(n):
        idx += list(range(i * d + d // 2, (i + 1) * d)) + list(range(i * d, i * d + d // 2))
    return np.array(idx, dtype=np.int32)


def _split_cols(w):
    out, off = {}, 0
    for n, s in zip(IN_NAMES, IN_SIZES):
        out[n] = w[..., off:off + s]
        off += s
    return out


def _pad_cols(a, left, total):
    pads = [(0, 0)] * (a.ndim - 1) + [(left, total - left - a.shape[-1])]
    return jnp.pad(a, pads)


def _tile_last(a, n):
    return jnp.concatenate([a] * n, axis=-1)


def _prepare_weights(w_in, mla_w_uq, mla_w_ukv):
    c = _split_cols(w_in)
    r64_4, r64_1, r32_8, r32_1 = _rot_idx(64, 4), _rot_idx(64, 1), _rot_idx(32, 8), _rot_idx(32, 1)
    parts = {
        "aq": c["a_q"], "ak": c["a_k"], "av": c["a_v"],
        "g": jnp.concatenate([c["a_g"], c["b_g"], c["c_g"], c["d_g"]], -1),
        "bcq": c["b_cq"], "bckv": c["b_ckv"],
        "bkr": _pad_cols(c["b_kr"], 64, 128), "bkr_rot": _pad_cols(c["b_kr"][..., r32_1], 64, 128),
        "cq": c["c_q"], "cq_rot": c["c_q"][..., r64_4],
        "ck": _pad_cols(c["c_k"], 0, 128), "ck_rot": _pad_cols(c["c_k"][..., r64_1], 0, 128),
        "cv": _pad_cols(c["c_v"], 0, 128),
        "ciq": c["c_iq"], "ciq_rot": c["c_iq"][..., r64_4],
        "cik": _pad_cols(c["c_ik"], 0, 128), "cik_rot": _pad_cols(c["c_ik"][..., r64_1], 0, 128),
        "ciw": _pad_cols(c["c_iw"], 0, 128),
        "dq": c["d_q"], "dq_rot": c["d_q"][..., r32_8], "dk": c["d_k"], "dk_rot": c["d_k"][..., r32_8],
        "dv": c["d_v"],
    }
    w = jnp.concatenate([parts[n] for n, _ in _SLABS], axis=-1).astype(BF16)

    depth = w_in.shape[0]
    uq = mla_w_uq.reshape(depth, 256, N_HEADS, MLA_QK)
    uq_x = _pad_cols(uq, 0, 128).reshape(depth, 256, 512)
    uq_r = _pad_cols(uq[..., MLA_NOPE:][..., r32_1], 64, 128).reshape(depth, 256, 512)
    wuq = jnp.concatenate([uq_x, uq_r], -1).astype(BF16)
    ukv = mla_w_ukv.reshape(depth, 128, N_HEADS, 128)
    uk = _pad_cols(ukv[..., :MLA_NOPE], 0, 128).reshape(depth, 128, 512)
    uv = ukv[..., MLA_NOPE:].reshape(depth, 128, 256)
    wukv = jnp.concatenate([uk, uv], -1).astype(BF16)
    return w, wuq, wukv


def _prepare_gains(ln_g, mla_q_norm_g, mla_kv_norm_g, mla_q_g, mla_k_g, dsa_q_g, dsa_k_g, diff_q_g, diff_k_g):
    r32, r64 = _rot_idx(32, 1), _rot_idx(64, 1)

    def row(a):
        return _pad_cols(a, 0, 1024)[:, None, :]

    sb = MLA_QK ** -0.5 * LOG2E
    sc = HEAD_DIM ** -0.5 * LOG2E
    sd = DIFF_DK ** -0.5 * LOG2E
    qb = _tile_last(_pad_cols(mla_q_g, 0, 128), 4) * sb
    qb_rot = _tile_last(_pad_cols(mla_q_g[:, MLA_NOPE:][:, r32], 64, 128), 4) * sb
    kb = _tile_last(_pad_cols(mla_k_g[:, :MLA_NOPE], 0, 128), 4)
    kr = _pad_cols(mla_k_g[:, MLA_NOPE:], 64, 128)
    kr_rot = _pad_cols(mla_k_g[:, MLA_NOPE:][:, r32], 64, 128)
    rows = [ln_g, mla_q_norm_g, mla_kv_norm_g, qb, qb_rot, kb, kr, kr_rot,
            _tile_last(dsa_q_g, 4) * sc, _tile_last(dsa_q_g[:, r64], 4) * sc,
            _tile_last(dsa_k_g, 2), _tile_last(dsa_k_g[:, r64], 2),
            _tile_last(diff_q_g, 8) * sd, _tile_last(diff_q_g[:, r32], 8) * sd,
            _tile_last(diff_k_g, 8), _tile_last(diff_k_g[:, r32], 8)]
    return jnp.concatenate([row(r.astype(F32)) for r in rows], axis=1)


def _rope_tables(seq):
    pos = jnp.arange(seq, dtype=F32)[:, None]

    def cs(d):
        inv = ROPE_THETA ** (-jnp.arange(0, d, 2, dtype=F32) / d)
        ang = pos * inv[None, :]
        cos, sin = jnp.cos(ang), jnp.sin(ang)
        return jnp.concatenate([cos, cos], -1), jnp.concatenate([-sin, sin], -1)

    c32, s32 = cs(32)
    c64, s64 = cs(64)
    one, zero = jnp.ones((seq, 64), F32), jnp.zeros((seq, 64), F32)
    z32 = jnp.zeros((seq, 32), F32)
    tb = jnp.stack([jnp.concatenate([one, c32, z32], -1), jnp.concatenate([zero, s32, z32], -1)])
    t64 = jnp.stack([_tile_last(c64, 2), _tile_last(s64, 2)])
    t32 = jnp.stack([_tile_last(c32, 4), _tile_last(s32, 4)])
    return jnp.stack([tb, t64, t32])


def _segment_mean_matrices():
    lane = np.arange(256)

    def seg(ids, sizes):
        same = ids[:, None] == ids[None, :]
        return np.where(same, 1.0 / sizes[None, :], 0.0)

    in128 = lane % 128
    ids_b = (lane // 128) * 3 + np.where(in128 < 64, 0, np.where(in128 < 96, 1, 2))
    sizes_b = np.where(in128 < 64, 64.0, 32.0)
    mats = [seg(ids_b, sizes_b), seg(lane // 64, np.full(256, 64.0)), seg(lane // 32, np.full(256, 32.0))]
    return jnp.asarray(np.stack(mats), dtype=BF16)


def _dot(a, b):
    return jnp.dot(a, b, preferred_element_type=F32)


def _segment_mean(sq, m):
    hi = sq.astype(BF16)
    lo = (sq - hi.astype(F32)).astype(BF16)
    return _dot(hi, m) + _dot(lo, m)


def _segment_mean_wide(sq, m):
    width = sq.shape[-1]
    if width == LANES:
        return _segment_mean(_tile_last(sq, 2), m)[:, :LANES]
    return jnp.concatenate([_segment_mean(sq[:, 256 * i:256 * (i + 1)], m) for i in range(width // 256)], axis=-1)


def _proj_kernel(x_ref, w_ref, wuq_ref, wukv_ref, gv_ref, seg_ref, tab_ref,
                 qa_ref, ka_ref, va_ref, gate_ref, qb_ref, kb_ref, vb_ref,
                 qc_ref, kc_ref, vc_ref, iq_ref, ik_ref, iw_ref, qd_ref, kd_ref, vd_ref):
    x = x_ref[...]
    r = lax.rsqrt(jnp.mean(x * x, axis=-1, keepdims=True) + EPS)
    h = (x * r * gv_ref[0:1, :]).astype(BF16)

    def proj(name):
        off, width = _SLAB[name]
        return _dot(h, w_ref[:, off:off + width])

    def gain(row, width):
        return gv_ref[row:row + 1, 0:width]

    def put_t(ref, val, rows=None):
        vt = val.T
        ref[0] = (vt if rows is None else vt[:rows]).astype(ref.dtype)

    seg_b, seg_64, seg_32 = seg_ref[0], seg_ref[1], seg_ref[2]
    cos_b, sin_b = tab_ref[0, 0], tab_ref[0, 1]
    cos_64, sin_64 = tab_ref[1, 0], tab_ref[1, 1]
    cos_32, sin_32 = tab_ref[2, 0], tab_ref[2, 1]

    def wide(tab, width):
        return _tile_last(tab, width // LANES) if width > LANES else tab

    def normed_rope(xv, xrot, seg, g_row, grot_row, cos, sin):
        width = xv.shape[-1]
        rr = lax.rsqrt(_segment_mean_wide(xv * xv, seg) + EPS)
        return rr * (xv * (gain(g_row, width) * wide(cos, width)) + xrot * (gain(grot_row, width) * wide(sin, width)))

    put_t(qa_ref, proj("aq") * 0.125)
    ka_ref[...] = proj("ak").astype(BF16)
    put_t(va_ref, proj("av"))

    g = proj("g")
    gate_ref[...] = g * jax.nn.sigmoid(g)

    cq = proj("bcq")
    cq = (cq * lax.rsqrt(jnp.mean(cq * cq, axis=-1, keepdims=True) + EPS) * gain(1, 256)).astype(BF16)
    ckv = proj("bckv")
    ckv = (ckv * lax.rsqrt(jnp.mean(ckv * ckv, axis=-1, keepdims=True) + EPS) * gain(2, 128)).astype(BF16)
    q2 = _dot(cq, wuq_ref[...])
    put_t(qb_ref, normed_rope(q2[:, :512], q2[:, 512:], seg_b, 3, 4, cos_b, sin_b))
    kv2 = _dot(ckv, wukv_ref[...])
    k_nope = kv2[:, :512]
    rk = lax.rsqrt(_segment_mean_wide(k_nope * k_nope, seg_b) + EPS)
    kr_out = normed_rope(proj("bkr"), proj("bkr_rot"), seg_b, 6, 7, cos_b, sin_b)
    kb_ref[...] = (rk * k_nope * gain(5, 512) + _tile_last(kr_out, 4)).astype(BF16)
    put_t(vb_ref, kv2[:, 512:])

    put_t(qc_ref, normed_rope(proj("cq"), proj("cq_rot"), seg_64, 8, 9, cos_64, sin_64))
    kc_ref[...] = normed_rope(proj("ck"), proj("ck_rot"), seg_64, 10, 11, cos_64, sin_64)[:, :HEAD_DIM].astype(BF16)
    put_t(vc_ref, proj("cv"), HEAD_DIM)
    put_t(iq_ref, (proj("ciq") * wide(cos_64, 256) + proj("ciq_rot") * wide(sin_64, 256)) * 0.125)
    ik_ref[...] = (proj("cik") * cos_64 + proj("cik_rot") * sin_64)[:, :HEAD_DIM].astype(BF16)
    put_t(iw_ref, proj("ciw") * 0.5, SUBLANES)

    put_t(qd_ref, normed_rope(proj("dq"), proj("dq_rot"), seg_32, 12, 13, cos_32, sin_32))
    kd_ref[...] = normed_rope(proj("dk"), proj("dk_rot"), seg_32, 14, 15, cos_32, sin_32).astype(BF16)
    put_t(vd_ref, proj("dv"))


_PROJ_OUT = (("qa", 256, BF16, True), ("ka", 256, BF16, False), ("va", 256, BF16, True), ("gate", 1024, F32, False),
             ("qb", 512, BF16, True), ("kb", 512, BF16, False), ("vb", 256, BF16, True),
             ("qc", 256, BF16, True), ("kc", 64, BF16, False), ("vc", 64, BF16, True), ("iq", 256, BF16, True),
             ("ik", 64, BF16, False), ("iw", SUBLANES, F32, True),
             ("qd", 256, BF16, True), ("kd", 256, BF16, False), ("vd", 256, BF16, True))


def _project(x2d, w, wuq, wukv, gv, seg, tabs, seq):
    rows = x2d.shape[0]
    n_tiles = rows // TILE
    n_seq = seq // TILE

    def const(shape):
        return pl.BlockSpec(shape, lambda i: (0,) * len(shape))

    in_specs = [pl.BlockSpec((TILE, D_MODEL), lambda i: (i, 0)),
                const(w.shape), const(wuq.shape), const(wukv.shape), const(gv.shape), const(seg.shape),
                pl.BlockSpec((3, 2, TILE, LANES), lambda i: (0, 0, i % n_seq, 0))]
    out_specs, out_shape = [], []
    for _, width, dt, transposed in _PROJ_OUT:
        if transposed:
            out_specs.append(pl.BlockSpec((1, width, TILE), lambda i: (i, 0, 0)))
            out_shape.append(jax.ShapeDtypeStruct((n_tiles, width, TILE), dt))
        else:
            out_specs.append(pl.BlockSpec((TILE, width), lambda i: (i, 0)))
            out_shape.append(jax.ShapeDtypeStruct((rows, width), dt))
    outs = pl.pallas_call(
        _proj_kernel, grid=(n_tiles,), in_specs=in_specs, out_specs=out_specs, out_shape=out_shape,
        compiler_params=pltpu.CompilerParams(dimension_semantics=("arbitrary",), vmem_limit_bytes=VMEM_LIMIT),
        name="proj",
    )(x2d, w, wuq, wukv, gv, seg, tabs)
    return {n: o for (n, _, _, _), o in zip(_PROJ_OUT, outs)}


def _out_kernel(x_ref, gate_ref, ya_ref, yb_ref, yc_ref, yd_ref, w_ref, o_ref):
    y = jnp.concatenate([ya_ref[...], yb_ref[...], yc_ref[...], yd_ref[...]], axis=-1) * gate_ref[...]
    o_ref[...] = x_ref[...] + _dot(y.astype(BF16), w_ref[...])


def _out_project(x2d, gate, ya, yb, yc, yd, w_out):
    rows = x2d.shape[0]
    row_spec = lambda width: pl.BlockSpec((TILE, width), lambda i: (i, 0))
    return pl.pallas_call(
        _out_kernel, grid=(rows // TILE,),
        in_specs=[row_spec(D_MODEL), row_spec(D_MODEL), row_spec(GROUP), row_spec(GROUP), row_spec(GROUP),
                  row_spec(GROUP), pl.BlockSpec((D_MODEL, D_MODEL), lambda i: (0, 0))],
        out_specs=row_spec(D_MODEL), out_shape=jax.ShapeDtypeStruct((rows, D_MODEL), F32),
        compiler_params=pltpu.CompilerParams(dimension_semantics=("arbitrary",), vmem_limit_bytes=VMEM_LIMIT),
        name="out_proj",
    )(x2d, gate, ya, yb, yc, yd, w_out)


def _tile_iotas():
    return (lax.broadcasted_iota(jnp.int32, (TILE, TILE), 0), lax.broadcasted_iota(jnp.int32, (TILE, TILE), 1))


def _chunk_visible():
    key, query = _tile_iotas()
    return (key >> CHUNK_SHIFT) <= (query >> CHUNK_SHIFT)


def _keep_rows(q, shift, group):
    row = lax.broadcasted_iota(jnp.int32, q.shape, 0)
    return jnp.where((row >> shift) == group, q, jnp.zeros_like(q))


def _online_softmax_step(ss, carry, acc_ref, v_tiles):
    ms, ls = carry
    n = len(ss)
    m_new = [jnp.maximum(ms[h], jnp.max(ss[h], axis=0, keepdims=True)) for h in range(n)]
    alpha = [jnp.exp2(ms[h] - m_new[h]) for h in range(n)]
    ps = [jnp.exp2(ss[h] - m_new[h]) for h in range(n)]
    l_new = [alpha[h] * ls[h] + jnp.sum(ps[h], axis=0, keepdims=True) for h in range(n)]
    pvs = [_dot(v_tiles[h], ps[h].astype(BF16)) for h in range(n)]
    for h in range(n):
        acc_ref[h] = alpha[h] * acc_ref[h] + pvs[h]
    return tuple(m_new), tuple(l_new)


def _softmax_init(n):
    return (tuple(jnp.full((1, TILE), MASKED, F32) for _ in range(n)),
            tuple(jnp.zeros((1, TILE), F32) for _ in range(n)))


def _attn_call(kernel, q_arrays, k_arrays, v_arrays, extra, batch, seq, scratch, name):
    nq = seq // TILE
    in_specs = ([pl.BlockSpec((1,) + a.shape[1:], lambda b, i: (b * nq + i, 0, 0)) for a in q_arrays]
                + [pl.BlockSpec((seq, a.shape[1]), lambda b, i: (b, 0)) for a in k_arrays]
                + [pl.BlockSpec((nq,) + a.shape[1:], lambda b, i: (b, 0, 0)) for a in v_arrays]
                + [pl.BlockSpec(a.shape, lambda b, i: (0,) * a.ndim) for a in extra])
    return pl.pallas_call(
        kernel, grid=(batch, nq), in_specs=in_specs,
        out_specs=pl.BlockSpec((TILE, GROUP), lambda b, i: (b * nq + i, 0)),
        out_shape=jax.ShapeDtypeStruct((batch * seq, GROUP), F32),
        scratch_shapes=scratch,
        compiler_params=pltpu.CompilerParams(dimension_semantics=("arbitrary", "arbitrary"),
                                             vmem_limit_bytes=VMEM_LIMIT),
        name=name,
    )(*q_arrays, *k_arrays, *v_arrays, *extra)


def _store_heads(o_ref, heads):
    o_ref[...] = jnp.concatenate(heads, axis=0).T


def _attn_a_kernel(q_ref, k_ref, v_ref, o_ref, acc_ref):
    i = pl.program_id(1)
    key, query = _tile_iotas()
    strict = key < query
    later = (query > key).astype(BF16)
    qs = [_keep_rows(q_ref[0, LANES * (h // 2):LANES * (h // 2 + 1), :], 6, h % 2) for h in range(N_HEADS)]
    acc_ref[...] = jnp.zeros_like(acc_ref)

    def tile(j, rs, on_diagonal):
        off = pl.multiple_of(j * TILE, TILE)
        heads = range(N_HEADS)
        zs = [_dot(k_ref[pl.ds(off, TILE), LANES * (h // 2):LANES * (h // 2 + 1)], qs[h]) for h in heads]
        sps = [jnp.maximum(z, 0.0) + jnp.log1p(jnp.exp(-jnp.abs(z))) for z in zs]
        lms = [-sp for sp in sps]
        if on_diagonal:
            lms = [jnp.where(strict, lm, 0.0) for lm in lms]
        cums = []
        for lm in lms:
            l1 = lm.astype(BF16)
            r1 = lm - l1.astype(F32)
            l2 = r1.astype(BF16)
            l3 = (r1 - l2.astype(F32)).astype(BF16)
            cums.append(_dot(later, l1) + _dot(later, l2) + _dot(later, l3))
        a_s = [jnp.exp(zs[h] - sps[h] + (rs[h] + cums[h])) for h in heads]
        if on_diagonal:
            a_s = [jnp.where(strict, a, 0.0) for a in a_s]
        pvs = [_dot(v_ref[j, HEAD_DIM * h:HEAD_DIM * (h + 1), :], a_s[h].astype(BF16)) for h in heads]
        for h in heads:
            acc_ref[h] += pvs[h]
        return tuple(rs[h] + jnp.sum(lms[h], axis=0, keepdims=True) for h in heads)

    def r_max(rs):
        return jnp.max(jnp.maximum(jnp.maximum(rs[0], rs[1]), jnp.maximum(rs[2], rs[3])))

    rs0 = tile(i, tuple(jnp.zeros((1, TILE), F32) for _ in range(N_HEADS)), True)

    def cond(state):
        j, _, rmax = state
        return jnp.logical_and(j >= 0, rmax > EXP_UNDERFLOW)

    def body(state):
        j, rs, _ = state
        rs = tile(j, rs, False)
        return j - 1, rs, r_max(rs)

    lax.while_loop(cond, body, (i - 1, rs0, r_max(rs0)))
    _store_heads(o_ref, [acc_ref[h] for h in range(N_HEADS)])


def _attn_b_kernel(q_ref, k_ref, v_ref, o_ref, acc_ref):
    i = pl.program_id(1)
    visible = _chunk_visible()
    qs = [q_ref[0, LANES * h:LANES * (h + 1), :] for h in range(N_HEADS)]
    acc_ref[...] = jnp.zeros_like(acc_ref)

    def tile(j, carry, on_diagonal):
        off = pl.multiple_of(j * TILE, TILE)
        ss = [_dot(k_ref[pl.ds(off, TILE), LANES * h:LANES * (h + 1)], qs[h]) for h in range(N_HEADS)]
        if on_diagonal:
            ss = [jnp.where(visible, s, MASKED) for s in ss]
        v_tiles = [v_ref[j, HEAD_DIM * h:HEAD_DIM * (h + 1), :] for h in range(N_HEADS)]
        return _online_softmax_step(ss, carry, acc_ref, v_tiles)

    carry = lax.fori_loop(0, i, lambda j, c: tile(j, c, False), _softmax_init(N_HEADS))
    _, ls = tile(i, carry, True)
    _store_heads(o_ref, [acc_ref[h] / ls[h] for h in range(N_HEADS)])


def _attn_d_kernel(q_ref, k_ref, v_ref, par_ref, gain_ref, o_ref, acc_ref):
    i = pl.program_id(1)
    visible = _chunk_visible()
    n_sub = 2 * N_HEADS
    qs = [_keep_rows(q_ref[0, LANES * (s // 4):LANES * (s // 4 + 1), :], 5, s % 4) for s in range(n_sub)]
    acc_ref[...] = jnp.zeros_like(acc_ref)

    def tile(j, carry, on_diagonal):
        off = pl.multiple_of(j * TILE, TILE)
        ss = [_dot(k_ref[pl.ds(off, TILE), LANES * (s_i // 4):LANES * (s_i // 4 + 1)], qs[s_i])
              for s_i in range(n_sub)]
        if on_diagonal:
            ss = [jnp.where(visible, s, MASKED) for s in ss]
        v_tiles = [v_ref[j, HEAD_DIM * (s_i // 2):HEAD_DIM * (s_i // 2 + 1), :] for s_i in range(n_sub)]
        return _online_softmax_step(ss, carry, acc_ref, v_tiles)

    carry = lax.fori_loop(0, i, lambda j, c: tile(j, c, False), _softmax_init(n_sub))
    _, ls = tile(i, carry, True)

    lam = (jnp.exp(jnp.sum(par_ref[0:1, :] * par_ref[1:2, :], axis=-1, keepdims=True))
           - jnp.exp(jnp.sum(par_ref[2:3, :] * par_ref[3:4, :], axis=-1, keepdims=True)) + par_ref[4:5, 0:1])
    ys = []
    for h in range(N_HEADS):
        o = acc_ref[2 * h] / ls[2 * h] - lam * (acc_ref[2 * h + 1] / ls[2 * h + 1])
        ms_o = jnp.mean(o * o, axis=0, keepdims=True)
        ys.append(o * lax.rsqrt(ms_o + EPS) * gain_ref[...])
    _store_heads(o_ref, ys)


def _ordered_key(score):
    bits = lax.bitcast_convert_type(score, jnp.int32)
    return jnp.where(bits < 0, jnp.int32(INT_MIN) - bits, bits)


def _attn_c_kernel(q_ref, iq_ref, iw_ref, k_ref, ik_ref, v_ref, o_ref, keys_ref, acc_ref, *, topk):
    i = pl.program_id(1)
    n_kv = i + 1
    visible = _chunk_visible()
    kf = float(topk)

    iqs = [iq_ref[0, HEAD_DIM * h:HEAD_DIM * (h + 1), :] for h in range(N_HEADS)]
    iws = [iw_ref[0, h:h + 1, :] for h in range(N_HEADS)]

    def score_tile(j, on_diagonal):
        ik_t = ik_ref[pl.ds(pl.multiple_of(j * TILE, TILE), TILE), :]
        score = iws[0] * jnp.maximum(_dot(ik_t, iqs[0]), 0.0)
        for h in range(1, N_HEADS):
            score = score + iws[h] * jnp.maximum(_dot(ik_t, iqs[h]), 0.0)
        key = _ordered_key(score)
        if on_diagonal:
            key = jnp.where(visible, key, jnp.int32(KEY_NEG_INF))
        keys_ref[j] = key

    def score_body(j, c):
        score_tile(j, False)
        return c

    lax.fori_loop(0, i, score_body, 0)
    score_tile(i, True)

    def count(pred):
        def body(j, acc):
            ind = jnp.where(pred(keys_ref[j]), 1.0, 0.0)
            return acc + jnp.sum(ind.reshape(TILE // SUBLANES, SUBLANES, TILE), axis=0)
        acc = lax.fori_loop(0, n_kv, body, jnp.zeros((SUBLANES, TILE), F32))
        return jnp.sum(acc, axis=0, keepdims=True)

    def bit_body(b, thr):
        cand = thr + jnp.left_shift(jnp.int32(1), 31 - b)
        return jnp.where(count(lambda kt: kt >= cand) >= kf, cand, thr)

    thr = lax.fori_loop(0, 32, bit_body, jnp.full((1, TILE), INT_MIN, jnp.int32))

    n_gt = count(lambda kt: kt > thr)
    n_ge = count(lambda kt: kt >= thr)
    need = kf - n_gt
    surplus = jnp.logical_and(n_ge - n_gt > need, thr > KEY_NEG_INF)
    any_surplus = jnp.max(jnp.where(surplus, 1.0, 0.0)) > 0.0

    @pl.when(any_surplus)
    def _():
        key, query = _tile_iotas()
        upto = (query <= key).astype(BF16)

        def body(j, seen):
            kt = keys_ref[j]
            tie = jnp.where(kt == thr, 1.0, 0.0)
            rank = seen + _dot(upto, tie.astype(BF16))
            keys_ref[j] = jnp.where(jnp.logical_and(kt == thr, rank > need), jnp.int32(INT_MIN), kt)
            return seen + jnp.sum(tie, axis=0, keepdims=True)

        lax.fori_loop(0, n_kv, body, jnp.zeros((1, TILE), F32))

    thr_sel = jnp.maximum(thr, jnp.int32(KEY_NEG_INF + 1))
    qs = [q_ref[0, HEAD_DIM * h:HEAD_DIM * (h + 1), :] for h in range(N_HEADS)]
    acc_ref[...] = jnp.zeros_like(acc_ref)

    def attn_body(j, carry):
        k_t = k_ref[pl.ds(pl.multiple_of(j * TILE, TILE), TILE), :]
        ss = [_dot(k_t, qs[h]) for h in range(N_HEADS)]
        sel = keys_ref[j] >= thr_sel
        ss = [jnp.where(sel, s, MASKED) for s in ss]
        return _online_softmax_step(ss, carry, acc_ref, [v_ref[j]] * N_HEADS)

    _, ls = lax.fori_loop(0, n_kv, attn_body, _softmax_init(N_HEADS))
    _store_heads(o_ref, [acc_ref[h] / ls[h] for h in range(N_HEADS)])


def _diff_params(lq1, lk1, lq2, lk2, subln_g, lam_init):
    par = jnp.concatenate([
        _pad_cols(jnp.stack([lq1, lk1, lq2, lk2]), 0, LANES),
        jnp.full((1, LANES), lam_init, F32),
        jnp.zeros((3, LANES), F32)], axis=0).astype(F32)
    gain = jnp.broadcast_to((subln_g * (1.0 - lam_init)).astype(F32)[:, None], (HEAD_DIM, TILE))
    return par, gain


def _layer_groups(x2d, batch, seq, w, wuq, wukv, gv, seg, tabs, dpar, topk):
    p = _project(x2d, w, wuq, wukv, gv, seg, tabs, seq)
    head_acc = pltpu.VMEM((N_HEADS, HEAD_DIM, TILE), F32)
    ya = _attn_call(_attn_a_kernel, [p["qa"]], [p["ka"]], [p["va"]], [], batch, seq, [head_acc], "attn_a")
    yb = _attn_call(_attn_b_kernel, [p["qb"]], [p["kb"]], [p["vb"]], [], batch, seq, [head_acc], "attn_b")
    yc = _attn_call(functools.partial(_attn_c_kernel, topk=topk), [p["qc"], p["iq"], p["iw"]],
                    [p["kc"], p["ik"]], [p["vc"]], [], batch, seq,
                    [pltpu.VMEM((seq // TILE, TILE, TILE), jnp.int32), head_acc], "attn_c")
    yd = _attn_call(_attn_d_kernel, [p["qd"]], [p["kd"]], [p["vd"]], list(dpar), batch, seq,
                    [pltpu.VMEM((2 * N_HEADS, HEAD_DIM, TILE), F32)], "attn_d")
    return p, ya, yb, yc, yd


def _layer(x2d, batch, seq, w, wuq, wukv, gv, seg, tabs, dpar, w_out, topk):
    p, ya, yb, yc, yd = _layer_groups(x2d, batch, seq, w, wuq, wukv, gv, seg, tabs, dpar, topk)
    return _out_project(x2d, p["gate"], ya, yb, yc, yd, w_out)


def kernel(x, ln_g, w_in, mla_q_norm_g, mla_kv_norm_g, mla_w_uq, mla_w_ukv, mla_q_g, mla_k_g, dsa_q_g, dsa_k_g,
           diff_q_g, diff_k_g, diff_lq1, diff_lk1, diff_lq2, diff_lk2, diff_subln_g, w_out):
    batch, seq, _ = x.shape
    depth = w_in.shape[0]
    assert seq % TILE == 0
    topk = min(DSA_TOPK_MAX, seq // 4)

    w, wuq, wukv = _prepare_weights(w_in, mla_w_uq, mla_w_ukv)
    gv = _prepare_gains(ln_g, mla_q_norm_g, mla_kv_norm_g, mla_q_g, mla_k_g, dsa_q_g, dsa_k_g, diff_q_g, diff_k_g)
    seg = _segment_mean_matrices()
    tabs = _rope_tables(seq)
    w_out_bf = w_out.astype(BF16)

    x2d = x.reshape(batch * seq, D_MODEL)
    for layer in range(depth):
        lam_init = 0.8 - 0.6 * math.exp(-0.3 * layer)
        dpar = _diff_params(diff_lq1[layer], diff_lk1[layer], diff_lq2[layer], diff_lk2[layer],
                            diff_subln_g[layer], lam_init)
        x2d = _layer(x2d, batch, seq, w[layer], wuq[layer], wukv[layer], gv[layer], seg, tabs, dpar,
                     w_out_bf[layer], topk)
    return x2d.reshape(batch, seq, D_MODEL)
```

```python
import functools
import math

import numpy as np
import jax
import jax.numpy as jnp
from jax import lax
from jax.experimental import pallas as pl
from jax.experimental.pallas import tpu as pltpu

F32 = jnp.float32
BF16 = jnp.bfloat16

D_MODEL = 1024
GROUP = 256
HEAD_DIM = 64
N_HEADS = 4
CHUNK_SHIFT = 6
ROPE_THETA = 10000.0
EPS = 1e-6
MLA_NOPE, MLA_ROPE, MLA_QK = 64, 32, 96
DSA_TOPK_MAX = 256
DIFF_DK = 32
LOG2E = 1.4426950408889634

IN_SIZES = (256, 256, 256, 256, 256, 128, 32, 256, 256, 64, 64, 256, 256, 64, 4, 256, 256, 256, 256)
IN_NAMES = ("a_q", "a_k", "a_v", "a_g", "b_cq", "b_ckv", "b_kr", "b_g", "c_q", "c_k", "c_v", "c_g",
            "c_iq", "c_ik", "c_iw", "d_q", "d_k", "d_v", "d_g")

LANES = 128
SUBLANES = 8
TILE = 256
VMEM_LIMIT = 56 * 1024 * 1024
MASKED = -1e30
INT_MIN = -2147483648
KEY_NEG_INF = -2139095040
EXP_UNDERFLOW = -105.0

_SLABS = (("aq", 256), ("ak", 256), ("av", 256), ("g", 1024), ("bcq", 256), ("bckv", 128),
          ("bkr", 128), ("bkr_rot", 128), ("cq", 256), ("cq_rot", 256), ("ck", 128), ("ck_rot", 128),
          ("cv", 128), ("ciq", 256), ("ciq_rot", 256), ("cik", 128), ("cik_rot", 128), ("ciw", 128),
          ("dq", 256), ("dq_rot", 256), ("dk", 256), ("dk_rot", 256), ("dv", 256))
_SLAB = {}
_off = 0
for _n, _w in _SLABS:
    _SLAB[_n] = (_off, _w)
    _off += _w
W_COLS = _off


def _rot_idx(d, n):
    idx = []
    for i in range(n):
        idx += list(range(i * d + d // 2, (i + 1) * d)) + list(range(i * d, i * d + d // 2))
    return np.array(idx, dtype=np.int32)


def _split_cols(w):
    out, off = {}, 0
    for n, s in zip(IN_NAMES, IN_SIZES):
        out[n] = w[..., off:off + s]
        off += s
    return out


def _pad_cols(a, left, total):
    pads = [(0, 0)] * (a.ndim - 1) + [(left, total - left - a.shape[-1])]
    return jnp.pad(a, pads)


def _tile_last(a, n):
    return jnp.concatenate([a] * n, axis=-1)


def _prepare_weights(w_in, mla_w_uq, mla_w_ukv):
    c = _split_cols(w_in)
    r64_4, r64_1, r32_8, r32_1 = _rot_idx(64, 4), _rot_idx(64, 1), _rot_idx(32, 8), _rot_idx(32, 1)
    parts = {
        "aq": c["a_q"], "ak": c["a_k"], "av": c["a_v"],
        "g": jnp.concatenate([c["a_g"], c["b_g"], c["c_g"], c["d_g"]], -1),
        "bcq": c["b_cq"], "bckv": c["b_ckv"],
        "bkr": _pad_cols(c["b_kr"], 64, 128), "bkr_rot": _pad_cols(c["b_kr"][..., r32_1], 64, 128),
        "cq": c["c_q"], "cq_rot": c["c_q"][..., r64_4],
        "ck": _pad_cols(c["c_k"], 0, 128), "ck_rot": _pad_cols(c["c_k"][..., r64_1], 0, 128),
        "cv": _pad_cols(c["c_v"], 0, 128),
        "ciq": c["c_iq"], "ciq_rot": c["c_iq"][..., r64_4],
        "cik": _pad_cols(c["c_ik"], 0, 128), "cik_rot": _pad_cols(c["c_ik"][..., r64_1], 0, 128),
        "ciw": _pad_cols(c["c_iw"], 0, 128),
        "dq": c["d_q"], "dq_rot": c["d_q"][..., r32_8], "dk": c["d_k"], "dk_rot": c["d_k"][..., r32_8],
        "dv": c["d_v"],
    }
    w = jnp.concatenate([parts[n] for n, _ in _SLABS], axis=-1).astype(BF16)

    depth = w_in.shape[0]
    uq = mla_w_uq.reshape(depth, 256, N_HEADS, MLA_QK)
    uq_x = _pad_cols(uq, 0, 128).reshape(depth, 256, 512)
    uq_r = _pad_cols(uq[..., MLA_NOPE:][..., r32_1], 64, 128).reshape(depth, 256, 512)
    wuq = jnp.concatenate([uq_x, uq_r], -1).astype(BF16)
    ukv = mla_w_ukv.reshape(depth, 128, N_HEADS, 128)
    uk = _pad_cols(ukv[..., :MLA_NOPE], 0, 128).reshape(depth, 128, 512)
    uv = ukv[..., MLA_NOPE:].reshape(depth, 128, 256)
    wukv = jnp.concatenate([uk, uv], -1).astype(BF16)
    return w, wuq, wukv


def _prepare_gains(ln_g, mla_q_norm_g, mla_kv_norm_g, mla_q_g, mla_k_g, dsa_q_g, dsa_k_g, diff_q_g, diff_k_g):
    r32, r64 = _rot_idx(32, 1), _rot_idx(64, 1)

    def row(a):
        return _pad_cols(a, 0, 1024)[:, None, :]

    sb = MLA_QK ** -0.5 * LOG2E
    sc = HEAD_DIM ** -0.5 * LOG2E
    sd = DIFF_DK ** -0.5 * LOG2E
    qb = _tile_last(_pad_cols(mla_q_g, 0, 128), 4) * sb
    qb_rot = _tile_last(_pad_cols(mla_q_g[:, MLA_NOPE:][:, r32], 64, 128), 4) * sb
    kb = _tile_last(_pad_cols(mla_k_g[:, :MLA_NOPE], 0, 128), 4)
    kr = _pad_cols(mla_k_g[:, MLA_NOPE:], 64, 128)
    kr_rot = _pad_cols(mla_k_g[:, MLA_NOPE:][:, r32], 64, 128)
    rows = [ln_g, mla_q_norm_g, mla_kv_norm_g, qb, qb_rot, kb, kr, kr_rot,
            _tile_last(dsa_q_g, 4) * sc, _tile_last(dsa_q_g[:, r64], 4) * sc,
            _tile_last(dsa_k_g, 2), _tile_last(dsa_k_g[:, r64], 2),
            _tile_last(diff_q_g, 8) * sd, _tile_last(diff_q_g[:, r32], 8) * sd,
            _tile_last(diff_k_g, 8), _tile_last(diff_k_g[:, r32], 8)]
    return jnp.concatenate([row(r.astype(F32)) for r in rows], axis=1)


def _rope_tables(seq):
    pos = jnp.arange(seq, dtype=F32)[:, None]

    def cs(d):
        inv = ROPE_THETA ** (-jnp.arange(0, d, 2, dtype=F32) / d)
        ang = pos * inv[None, :]
        cos, sin = jnp.cos(ang), jnp.sin(ang)
        return jnp.concatenate([cos, cos], -1), jnp.concatenate([-sin, sin], -1)

    c32, s32 = cs(32)
    c64, s64 = cs(64)
    one, zero = jnp.ones((seq, 64), F32), jnp.zeros((seq, 64), F32)
    z32 = jnp.zeros((seq, 32), F32)
    tb = jnp.stack([jnp.concatenate([one, c32, z32], -1), jnp.concatenate([zero, s32, z32], -1)])
    t64 = jnp.stack([_tile_last(c64, 2), _tile_last(s64, 2)])
    t32 = jnp.stack([_tile_last(c32, 4), _tile_last(s32, 4)])
    return jnp.stack([tb, t64, t32])


def _segment_mean_matrices():
    lane = np.arange(256)

    def seg(ids, sizes):
        same = ids[:, None] == ids[None, :]
        return np.where(same, 1.0 / sizes[None, :], 0.0)

    in128 = lane % 128
    ids_b = (lane // 128) * 3 + np.where(in128 < 64, 0, np.where(in128 < 96, 1, 2))
    sizes_b = np.where(in128 < 64, 64.0, 32.0)
    mats = [seg(ids_b, sizes_b), seg(lane // 64, np.full(256, 64.0)), seg(lane // 32, np.full(256, 32.0))]
    return jnp.asarray(np.stack(mats), dtype=BF16)


def _dot(a, b):
    return jnp.dot(a, b, preferred_element_type=F32)


def _segment_mean(sq, m):
    hi = sq.astype(BF16)
    lo = (sq - hi.astype(F32)).astype(BF16)
    return _dot(hi, m) + _dot(lo, m)


def _segment_mean_wide(sq, m):
    width = sq.shape[-1]
    if width == LANES:
        return _segment_mean(_tile_last(sq, 2), m)[:, :LANES]
    return jnp.concatenate([_segment_mean(sq[:, 256 * i:256 * (i + 1)], m) for i in range(width // 256)], axis=-1)


def _proj_kernel(x_ref, w_ref, wuq_ref, wukv_ref, gv_ref, seg_ref, tab_ref,
                 qa_ref, ka_ref, va_ref, gate_ref, qb_ref, kb_ref, vb_ref,
                 qc_ref, kc_ref, vc_ref, iq_ref, ik_ref, iw_ref, qd_ref, kd_ref, vd_ref):
    x = x_ref[...]
    r = lax.rsqrt(jnp.mean(x * x, axis=-1, keepdims=True) + EPS)
    h = (x * r * gv_ref[0:1, :]).astype(BF16)

    def proj(name):
        off, width = _SLAB[name]
        return _dot(h, w_ref[:, off:off + width])

    def gain(row, width):
        return gv_ref[row:row + 1, 0:width]

    def put_t(ref, val, rows=None):
        vt = val.T
        ref[0] = (vt if rows is None else vt[:rows]).astype(ref.dtype)

    seg_b, seg_64, seg_32 = seg_ref[0], seg_ref[1], seg_ref[2]
    cos_b, sin_b = tab_ref[0, 0], tab_ref[0, 1]
    cos_64, sin_64 = tab_ref[1, 0], tab_ref[1, 1]
    cos_32, sin_32 = tab_ref[2, 0], tab_ref[2, 1]

    def wide(tab, width):
        return _tile_last(tab, width // LANES) if width > LANES else tab

    def normed_rope(xv, xrot, seg, g_row, grot_row, cos, sin):
        width = xv.shape[-1]
        rr = lax.rsqrt(_segment_mean_wide(xv * xv, seg) + EPS)
        return rr * (xv * (gain(g_row, width) * wide(cos, width)) + xrot * (gain(grot_row, width) * wide(sin, width)))

    put_t(qa_ref, proj("aq") * 0.125)
    ka_ref[...] = proj("ak").astype(BF16)
    put_t(va_ref, proj("av"))

    g = proj("g")
    gate_ref[...] = g * jax.nn.sigmoid(g)

    cq = proj("bcq")
    cq = (cq * lax.rsqrt(jnp.mean(cq * cq, axis=-1, keepdims=True) + EPS) * gain(1, 256)).astype(BF16)
    ckv = proj("bckv")
    ckv = (ckv * lax.rsqrt(jnp.mean(ckv * ckv, axis=-1, keepdims=True) + EPS) * gain(2, 128)).astype(BF16)
    q2 = _dot(cq, wuq_ref[...])
    put_t(qb_ref, normed_rope(q2[:, :512], q2[:, 512:], seg_b, 3, 4, cos_b, sin_b))
    kv2 = _dot(ckv, wukv_ref[...])
    k_nope = kv2[:, :512]
    rk = lax.rsqrt(_segment_mean_wide(k_nope * k_nope, seg_b) + EPS)
    kr_out = normed_rope(proj("bkr"), proj("bkr_rot"), seg_b, 6, 7, cos_b, sin_b)
    kb_ref[...] = (rk * k_nope * gain(5, 512) + _tile_last(kr_out, 4)).astype(BF16)
    put_t(vb_ref, kv2[:, 512:])

    put_t(qc_ref, normed_rope(proj("cq"), proj("cq_rot"), seg_64, 8, 9, cos_64, sin_64))
    kc_ref[...] = normed_rope(proj("ck"), proj("ck_rot"), seg_64, 10, 11, cos_64, sin_64)[:, :HEAD_DIM].astype(BF16)
    put_t(vc_ref, proj("cv"), HEAD_DIM)
    put_t(iq_ref, (proj("ciq") * wide(cos_64, 256) + proj("ciq_rot") * wide(sin_64, 256)) * 0.125)
    ik_ref[...] = (proj("cik") * cos_64 + proj("cik_rot") * sin_64)[:, :HEAD_DIM].astype(BF16)
    put_t(iw_ref, proj("ciw") * 0.5, SUBLANES)

    put_t(qd_ref, normed_rope(proj("dq"), proj("dq_rot"), seg_32, 12, 13, cos_32, sin_32))
    kd_ref[...] = normed_rope(proj("dk"), proj("dk_rot"), seg_32, 14, 15, cos_32, sin_32).astype(BF16)
    put_t(vd_ref, proj("dv"))


_PROJ_OUT = (("qa", 256, BF16, True), ("ka", 256, BF16, False), ("va", 256, BF16, True), ("gate", 1024, F32, False),
             ("qb", 512, BF16, True), ("kb", 512, BF16, False), ("vb", 256, BF16, True),
             ("qc", 256, BF16, True), ("kc", 64, BF16, False), ("vc", 64, BF16, True), ("iq", 256, BF16, True),
             ("ik", 64, BF16, False), ("iw", SUBLANES, F32, True),
             ("qd", 256, BF16, True), ("kd", 256, BF16, False), ("vd", 256, BF16, True))


def _project(x2d, w, wuq, wukv, gv, seg, tabs, seq):
    rows = x2d.shape[0]
    n_tiles = rows // TILE
    n_seq = seq // TILE

    def const(shape):
        return pl.BlockSpec(shape, lambda i: (0,) * len(shape))

    in_specs = [pl.BlockSpec((TILE, D_MODEL), lambda i: (i, 0)),
                const(w.shape), const(wuq.shape), const(wukv.shape), const(gv.shape), const(seg.shape),
                pl.BlockSpec((3, 2, TILE, LANES), lambda i: (0, 0, i % n_seq, 0))]
    out_specs, out_shape = [], []
    for _, width, dt, transposed in _PROJ_OUT:
        if transposed:
            out_specs.append(pl.BlockSpec((1, width, TILE), lambda i: (i, 0, 0)))
            out_shape.append(jax.ShapeDtypeStruct((n_tiles, width, TILE), dt))
        else:
            out_specs.append(pl.BlockSpec((TILE, width), lambda i: (i, 0)))
            out_shape.append(jax.ShapeDtypeStruct((rows, width), dt))
    outs = pl.pallas_call(
        _proj_kernel, grid=(n_tiles,), in_specs=in_specs, out_specs=out_specs, out_shape=out_shape,
        compiler_params=pltpu.CompilerParams(dimension_semantics=("arbitrary",), vmem_limit_bytes=VMEM_LIMIT),
        name="proj",
    )(x2d, w, wuq, wukv, gv, seg, tabs)
    return {n: o for (n, _, _, _), o in zip(_PROJ_OUT, outs)}


def _out_kernel(x_ref, gate_ref, ya_ref, yb_ref, yc_ref, yd_ref, w_ref, o_ref):
    y = jnp.concatenate([ya_ref[...], yb_ref[...], yc_ref[...], yd_ref[...]], axis=-1) * gate_ref[...]
    o_ref[...] = x_ref[...] + _dot(y.astype(BF16), w_ref[...])


def _out_project(x2d, gate, ya, yb, yc, yd, w_out):
    rows = x2d.shape[0]
    row_spec = lambda width: pl.BlockSpec((TILE, width), lambda i: (i, 0))
    return pl.pallas_call(
        _out_kernel, grid=(rows // TILE,),
        in_specs=[row_spec(D_MODEL), row_spec(D_MODEL), row_spec(GROUP), row_spec(GROUP), row_spec(GROUP),
                  row_spec(GROUP), pl.BlockSpec((D_MODEL, D_MODEL), lambda i: (0, 0))],
        out_specs=row_spec(D_MODEL), out_shape=jax.ShapeDtypeStruct((rows, D_MODEL), F32),
        compiler_params=pltpu.CompilerParams(dimension_semantics=("arbitrary",), vmem_limit_bytes=VMEM_LIMIT),
        name="out_proj",
    )(x2d, gate, ya, yb, yc, yd, w_out)


def _tile_iotas():
    return (lax.broadcasted_iota(jnp.int32, (TILE, TILE), 0), lax.broadcasted_iota(jnp.int32, (TILE, TILE), 1))


def _chunk_visible():
    key, query = _tile_iotas()
    return (key >> CHUNK_SHIFT) <= (query >> CHUNK_SHIFT)


def _keep_rows(q, shift, group):
    row = lax.broadcasted_iota(jnp.int32, q.shape, 0)
    return jnp.where((row >> shift) == group, q, jnp.zeros_like(q))


def _online_softmax_step(ss, carry, acc_ref, v_tiles):
    ms, ls = carry
    n = len(ss)
    m_new = [jnp.maximum(ms[h], jnp.max(ss[h], axis=0, keepdims=True)) for h in range(n)]
    alpha = [jnp.exp2(ms[h] - m_new[h]) for h in range(n)]
    ps = [jnp.exp2(ss[h] - m_new[h]) for h in range(n)]
    l_new = [alpha[h] * ls[h] + jnp.sum(ps[h], axis=0, keepdims=True) for h in range(n)]
    pvs = []
    for h in range(n):
        pb = ps[h].astype(BF16)
        if isinstance(v_tiles[h], (list, tuple)):
            pv = _dot(v_tiles[h][0], pb[:TILE])
            for t, v_t in enumerate(v_tiles[h][1:], start=1):
                pv = pv + _dot(v_t, pb[TILE * t:TILE * (t + 1)])
            pvs.append(pv)
        else:
            pvs.append(_dot(v_tiles[h], pb))
    for h in range(n):
        acc_ref[h] = alpha[h] * acc_ref[h] + pvs[h]
    return tuple(m_new), tuple(l_new)


def _softmax_init(n):
    return (tuple(jnp.full((1, TILE), MASKED, F32) for _ in range(n)),
            tuple(jnp.zeros((1, TILE), F32) for _ in range(n)))


def _attn_call(kernel, q_arrays, k_arrays, v_arrays, extra, batch, seq, scratch, name):
    nq = seq // TILE
    in_specs = ([pl.BlockSpec((1,) + a.shape[1:], lambda b, i: (b * nq + i, 0, 0)) for a in q_arrays]
                + [pl.BlockSpec((seq, a.shape[1]), lambda b, i: (b, 0)) for a in k_arrays]
                + [pl.BlockSpec((nq,) + a.shape[1:], lambda b, i: (b, 0, 0)) for a in v_arrays]
                + [pl.BlockSpec(a.shape, lambda b, i: (0,) * a.ndim) for a in extra])
    return pl.pallas_call(
        kernel, grid=(batch, nq), in_specs=in_specs,
        out_specs=pl.BlockSpec((TILE, GROUP), lambda b, i: (b * nq + i, 0)),
        out_shape=jax.ShapeDtypeStruct((batch * seq, GROUP), F32),
        scratch_shapes=scratch,
        compiler_params=pltpu.CompilerParams(dimension_semantics=("arbitrary", "arbitrary"),
                                             vmem_limit_bytes=VMEM_LIMIT),
        name=name,
    )(*q_arrays, *k_arrays, *v_arrays, *extra)


def _store_heads(o_ref, heads):
    o_ref[...] = jnp.concatenate(heads, axis=0).T


def _attn_a_kernel(q_ref, k_ref, v_ref, o_ref, acc_ref):
    i = pl.program_id(1)
    key, query = _tile_iotas()
    strict = key < query
    later = (query > key).astype(BF16)
    qs = [_keep_rows(q_ref[0, LANES * (h // 2):LANES * (h // 2 + 1), :], 6, h % 2) for h in range(N_HEADS)]
    acc_ref[...] = jnp.zeros_like(acc_ref)

    def tile(j, rs, on_diagonal):
        off = pl.multiple_of(j * TILE, TILE)
        heads = range(N_HEADS)
        zs = [_dot(k_ref[pl.ds(off, TILE), LANES * (h // 2):LANES * (h // 2 + 1)], qs[h]) for h in heads]
        sps = [jnp.maximum(z, 0.0) + jnp.log1p(jnp.exp(-jnp.abs(z))) for z in zs]
        lms = [-sp for sp in sps]
        if on_diagonal:
            lms = [jnp.where(strict, lm, 0.0) for lm in lms]
        cums = []
        for lm in lms:
            l1 = lm.astype(BF16)
            r1 = lm - l1.astype(F32)
            l2 = r1.astype(BF16)
            l3 = (r1 - l2.astype(F32)).astype(BF16)
            cums.append(_dot(later, l1) + _dot(later, l2) + _dot(later, l3))
        a_s = [jnp.exp(zs[h] - sps[h] + (rs[h] + cums[h])) for h in heads]
        if on_diagonal:
            a_s = [jnp.where(strict, a, 0.0) for a in a_s]
        pvs = [_dot(v_ref[j, HEAD_DIM * h:HEAD_DIM * (h + 1), :], a_s[h].astype(BF16)) for h in heads]
        for h in heads:
            acc_ref[h] += pvs[h]
        return tuple(rs[h] + jnp.sum(lms[h], axis=0, keepdims=True) for h in heads)

    def r_max(rs):
        return jnp.max(jnp.maximum(jnp.maximum(rs[0], rs[1]), jnp.maximum(rs[2], rs[3])))

    rs0 = tile(i, tuple(jnp.zeros((1, TILE), F32) for _ in range(N_HEADS)), True)

    def cond(state):
        j, _, rmax = state
        return jnp.logical_and(j >= 0, rmax > EXP_UNDERFLOW)

    def body(state):
        j, rs, _ = state
        rs = tile(j, rs, False)
        return j - 1, rs, r_max(rs)

    lax.while_loop(cond, body, (i - 1, rs0, r_max(rs0)))
    _store_heads(o_ref, [acc_ref[h] for h in range(N_HEADS)])


def _attn_b_kernel(q_ref, k_ref, v_ref, o_ref, acc_ref):
    i = pl.program_id(1)
    visible = _chunk_visible()
    qs = [q_ref[0, LANES * h:LANES * (h + 1), :] for h in range(N_HEADS)]
    acc_ref[...] = jnp.zeros_like(acc_ref)

    def tile(j, carry, on_diagonal):
        off = pl.multiple_of(j * TILE, TILE)
        ss = [_dot(k_ref[pl.ds(off, TILE), LANES * h:LANES * (h + 1)], qs[h]) for h in range(N_HEADS)]
        if on_diagonal:
            ss = [jnp.where(visible, s, MASKED) for s in ss]
        v_tiles = [v_ref[j, HEAD_DIM * h:HEAD_DIM * (h + 1), :] for h in range(N_HEADS)]
        return _online_softmax_step(ss, carry, acc_ref, v_tiles)

    def tile_pair(jj, carry):
        off = pl.multiple_of(jj * (2 * TILE), 2 * TILE)
        ss = [_dot(k_ref[pl.ds(off, 2 * TILE), LANES * h:LANES * (h + 1)], qs[h]) for h in range(N_HEADS)]
        v_tiles = [[v_ref[2 * jj + u, HEAD_DIM * h:HEAD_DIM * (h + 1), :] for u in range(2)] for h in range(N_HEADS)]
        return _online_softmax_step(ss, carry, acc_ref, v_tiles)

    carry = lax.fori_loop(0, i // 2, tile_pair, _softmax_init(N_HEADS))
    carry = lax.cond(i % 2 == 1, lambda c: tile(i - 1, c, False), lambda c: c, carry)
    _, ls = tile(i, carry, True)
    _store_heads(o_ref, [acc_ref[h] / ls[h] for h in range(N_HEADS)])


def _attn_d_kernel(q_ref, k_ref, v_ref, par_ref, gain_ref, o_ref, acc_ref):
    i = pl.program_id(1)
    visible = _chunk_visible()
    n_sub = 2 * N_HEADS
    qs = [_keep_rows(q_ref[0, LANES * (s // 4):LANES * (s // 4 + 1), :], 5, s % 4) for s in range(n_sub)]
    acc_ref[...] = jnp.zeros_like(acc_ref)

    def tile(j, carry, on_diagonal):
        off = pl.multiple_of(j * TILE, TILE)
        ss = [_dot(k_ref[pl.ds(off, TILE), LANES * (s_i // 4):LANES * (s_i // 4 + 1)], qs[s_i])
              for s_i in range(n_sub)]
        if on_diagonal:
            ss = [jnp.where(visible, s, MASKED) for s in ss]
        v_tiles = [v_ref[j, HEAD_DIM * (s_i // 2):HEAD_DIM * (s_i // 2 + 1), :] for s_i in range(n_sub)]
        return _online_softmax_step(ss, carry, acc_ref, v_tiles)

    carry = lax.fori_loop(0, i, lambda j, c: tile(j, c, False), _softmax_init(n_sub))
    _, ls = tile(i, carry, True)

    lam = (jnp.exp(jnp.sum(par_ref[0:1, :] * par_ref[1:2, :], axis=-1, keepdims=True))
           - jnp.exp(jnp.sum(par_ref[2:3, :] * par_ref[3:4, :], axis=-1, keepdims=True)) + par_ref[4:5, 0:1])
    ys = []
    for h in range(N_HEADS):
        o = acc_ref[2 * h] / ls[2 * h] - lam * (acc_ref[2 * h + 1] / ls[2 * h + 1])
        ms_o = jnp.mean(o * o, axis=0, keepdims=True)
        ys.append(o * lax.rsqrt(ms_o + EPS) * gain_ref[...])
    _store_heads(o_ref, ys)


def _ordered_key(score):
    bits = lax.bitcast_convert_type(score, jnp.int32)
    return jnp.where(bits < 0, jnp.int32(INT_MIN) - bits, bits)


def _attn_c_kernel(q_ref, iq_ref, iw_ref, k_ref, ik_ref, v_ref, o_ref, keys_ref, acc_ref, *, topk):
    i = pl.program_id(1)
    n_kv = i + 1
    n_pairs = (n_kv + 1) // 2
    visible = _chunk_visible()
    kf = float(topk)

    iqs = [iq_ref[0, HEAD_DIM * h:HEAD_DIM * (h + 1), :] for h in range(N_HEADS)]
    iws = [iw_ref[0, h:h + 1, :] for h in range(N_HEADS)]

    def score_tile(j, on_diagonal):
        ik_t = ik_ref[pl.ds(pl.multiple_of(j * TILE, TILE), TILE), :]
        logits = [_dot(ik_t, iqs[h]) for h in range(N_HEADS)]
        score = iws[0] * jnp.maximum(logits[0], 0.0)
        for h in range(1, N_HEADS):
            score = score + iws[h] * jnp.maximum(logits[h], 0.0)
        key = _ordered_key(score)
        if on_diagonal:
            key = jnp.where(visible, key, jnp.int32(KEY_NEG_INF))
        keys_ref[j] = key
        return jnp.max(key.reshape(TILE // SUBLANES, SUBLANES, TILE), axis=0)

    top = lax.fori_loop(0, i, lambda j, m: jnp.maximum(m, score_tile(j, False)),
                        jnp.full((SUBLANES, TILE), INT_MIN, jnp.int32))
    top = jnp.max(jnp.maximum(top, score_tile(i, True)), axis=0, keepdims=True)

    @pl.when(n_kv % 2 == 1)
    def _():
        keys_ref[n_kv] = jnp.full((TILE, TILE), INT_MIN, jnp.int32)

    def count_at_least(probe):
        def body(jj, acc):
            for u in range(2):
                ind = jnp.where(keys_ref[2 * jj + u] >= probe, 1.0, 0.0)
                acc = acc + jnp.sum(ind.reshape(TILE // SUBLANES, SUBLANES, TILE), axis=0)
            return acc
        acc = lax.fori_loop(0, n_pairs, body, jnp.zeros((SUBLANES, TILE), F32))
        return jnp.sum(acc, axis=0, keepdims=True)

    def open_count(lo, hi):
        return jnp.sum(jnp.where(hi > lo + 1, 1.0, 0.0))

    def search_body(state):
        lo, hi, c_lo, c_hi, _ = state
        is_open = hi > lo + 1
        probe = (lo >> 1) + (hi >> 1) + (lo & hi & 1)
        probe = jnp.where(jnp.logical_and(lo < 0, hi > 0), 0, probe)
        probe = jnp.where(jnp.logical_and(lo == 0, hi > 1), 1, probe)
        c = count_at_least(probe)
        up = jnp.logical_and(c >= kf, is_open)
        down = jnp.logical_and(c < kf, is_open)
        lo = jnp.where(up, probe, lo)
        c_lo = jnp.where(up, c, c_lo)
        hi = jnp.where(down, probe, hi)
        c_hi = jnp.where(down, c, c_hi)
        exact = jnp.logical_and(c == kf, is_open)
        hi = jnp.where(exact, lo + 1, hi)
        c_hi = jnp.where(exact, 0.0, c_hi)
        return lo, hi, c_lo, c_hi, open_count(lo, hi)

    lo0 = jnp.full((1, TILE), INT_MIN, jnp.int32)
    hi0 = top + 1
    n_keys = (n_kv * TILE).astype(F32)
    state0 = (lo0, hi0, jnp.zeros((1, TILE), F32) + n_keys, jnp.zeros((1, TILE), F32), open_count(lo0, hi0))
    thr, _, n_ge, n_gt, _ = lax.while_loop(lambda st: st[4] > 0.0, search_body, state0)

    need = kf - n_gt
    surplus = jnp.logical_and(n_ge - n_gt > need, thr > KEY_NEG_INF)
    any_surplus = jnp.max(jnp.where(surplus, 1.0, 0.0)) > 0.0

    @pl.when(any_surplus)
    def _():
        key, query = _tile_iotas()
        upto = (query <= key).astype(BF16)

        def body(j, seen):
            kt = keys_ref[j]
            tie = jnp.where(kt == thr, 1.0, 0.0)
            rank = seen + _dot(upto, tie.astype(BF16))
            keys_ref[j] = jnp.where(jnp.logical_and(kt == thr, rank > need), jnp.int32(INT_MIN), kt)
            return seen + jnp.sum(tie, axis=0, keepdims=True)

        lax.fori_loop(0, n_kv, body, jnp.zeros((1, TILE), F32))

    thr_sel = jnp.maximum(thr, jnp.int32(KEY_NEG_INF + 1))
    qs = [q_ref[0, HEAD_DIM * h:HEAD_DIM * (h + 1), :] for h in range(N_HEADS)]
    acc_ref[...] = jnp.zeros_like(acc_ref)

    def attn_body(jj, carry):
        k_t = k_ref[pl.ds(pl.multiple_of(jj * (2 * TILE), 2 * TILE), 2 * TILE), :]
        ss = [_dot(k_t, qs[h]) for h in range(N_HEADS)]
        sel = keys_ref[pl.ds(2 * jj, 2)].reshape(2 * TILE, TILE) >= thr_sel
        ss = [jnp.where(sel, s, MASKED) for s in ss]
        return _online_softmax_step(ss, carry, acc_ref, [[v_ref[2 * jj], v_ref[2 * jj + 1]]] * N_HEADS)

    _, ls = lax.fori_loop(0, n_pairs, attn_body, _softmax_init(N_HEADS))
    _store_heads(o_ref, [acc_ref[h] / ls[h] for h in range(N_HEADS)])


def _diff_params(lq1, lk1, lq2, lk2, subln_g, lam_init):
    par = jnp.concatenate([
        _pad_cols(jnp.stack([lq1, lk1, lq2, lk2]), 0, LANES),
        jnp.full((1, LANES), lam_init, F32),
        jnp.zeros((3, LANES), F32)], axis=0).astype(F32)
    gain = jnp.broadcast_to((subln_g * (1.0 - lam_init)).astype(F32)[:, None], (HEAD_DIM, TILE))
    return par, gain


def _layer_groups(x2d, batch, seq, w, wuq, wukv, gv, seg, tabs, dpar, topk):
    p = _project(x2d, w, wuq, wukv, gv, seg, tabs, seq)
    head_acc = pltpu.VMEM((N_HEADS, HEAD_DIM, TILE), F32)
    ya = _attn_call(_attn_a_kernel, [p["qa"]], [p["ka"]], [p["va"]], [], batch, seq, [head_acc], "attn_a")
    yb = _attn_call(_attn_b_kernel, [p["qb"]], [p["kb"]], [p["vb"]], [], batch, seq, [head_acc], "attn_b")
    yc = _attn_call(functools.partial(_attn_c_kernel, topk=topk), [p["qc"], p["iq"], p["iw"]],
                    [p["kc"], p["ik"]], [p["vc"]], [], batch, seq,
                    [pltpu.VMEM((seq // TILE + 1, TILE, TILE), jnp.int32), head_acc], "attn_c")
    yd = _attn_call(_attn_d_kernel, [p["qd"]], [p["kd"]], [p["vd"]], list(dpar), batch, seq,
                    [pltpu.VMEM((2 * N_HEADS, HEAD_DIM, TILE), F32)], "attn_d")
    return p, ya, yb, yc, yd


def _layer(x2d, batch, seq, w, wuq, wukv, gv, seg, tabs, dpar, w_out, topk):
    p, ya, yb, yc, yd = _layer_groups(x2d, batch, seq, w, wuq, wukv, gv, seg, tabs, dpar, topk)
    return _out_project(x2d, p["gate"], ya, yb, yc, yd, w_out)


def kernel(x, ln_g, w_in, mla_q_norm_g, mla_kv_norm_g, mla_w_uq, mla_w_ukv, mla_q_g, mla_k_g, dsa_q_g, dsa_k_g,
           diff_q_g, diff_k_g, diff_lq1, diff_lk1, diff_lq2, diff_lk2, diff_subln_g, w_out):
    batch, seq, _ = x.shape
    depth = w_in.shape[0]
    assert seq % (2 * TILE) == 0
    topk = min(DSA_TOPK_MAX, seq // 4)

    w, wuq, wukv = _prepare_weights(w_in, mla_w_uq, mla_w_ukv)
    gv = _prepare_gains(ln_g, mla_q_norm_g, mla_kv_norm_g, mla_q_g, mla_k_g, dsa_q_g, dsa_k_g, diff_q_g, diff_k_g)
    seg = _segment_mean_matrices()
    tabs = _rope_tables(seq)
    w_out_bf = w_out.astype(BF16)

    x2d = x.reshape(batch * seq, D_MODEL)
    for layer in range(depth):
        lam_init = 0.8 - 0.6 * math.exp(-0.3 * layer)
        dpar = _diff_params(diff_lq1[layer], diff_lk1[layer], diff_lq2[layer], diff_lk2[layer],
                            diff_subln_g[layer], lam_init)
        x2d = _layer(x2d, batch, seq, w[layer], wuq[layer], wukv[layer], gv[layer], seg, tabs, dpar,
                     w_out_bf[layer], topk)
    return x2d.reshape(batch, seq, D_MODEL)
```

```python
import functools
import math

import numpy as np
import jax
import jax.numpy as jnp
from jax import lax
from jax.experimental import pallas as pl
from jax.experimental.pallas import tpu as pltpu

F32 = jnp.float32
BF16 = jnp.bfloat16

D_MODEL = 1024
GROUP = 256
HEAD_DIM = 64
N_HEADS = 4
CHUNK_SHIFT = 6
ROPE_THETA = 10000.0
EPS = 1e-6
MLA_NOPE, MLA_ROPE, MLA_QK = 64, 32, 96
DSA_TOPK_MAX = 256
DIFF_DK = 32
LOG2E = 1.4426950408889634

IN_SIZES = (256, 256, 256, 256, 256, 128, 32, 256, 256, 64, 64, 256, 256, 64, 4, 256, 256, 256, 256)
IN_NAMES = ("a_q", "a_k", "a_v", "a_g", "b_cq", "b_ckv", "b_kr", "b_g", "c_q", "c_k", "c_v", "c_g",
            "c_iq", "c_ik", "c_iw", "d_q", "d_k", "d_v", "d_g")

LANES = 128
SUBLANES = 8
TILE = 256
VMEM_LIMIT = 56 * 1024 * 1024
MASKED = -1e30
INT_MIN = -2147483648
KEY_NEG_INF = -2139095040
BINADE = 1 << 23
GALLOP_STEPS = 3
EXP_UNDERFLOW = -105.0

_SLABS = (("aq", 256), ("ak", 256), ("av", 256), ("g", 1024), ("bcq", 256), ("bckv", 128),
          ("bkr", 128), ("bkr_rot", 128), ("cq", 256), ("cq_rot", 256), ("ck", 128), ("ck_rot", 128),
          ("cv", 128), ("ciq", 256), ("ciq_rot", 256), ("cik", 128), ("cik_rot", 128), ("ciw", 128),
          ("dq", 256), ("dq_rot", 256), ("dk", 256), ("dk_rot", 256), ("dv", 256))
_SLAB = {}
_off = 0
for _n, _w in _SLABS:
    _SLAB[_n] = (_off, _w)
    _off += _w
W_COLS = _off


def _rot_idx(d, n):
    idx = []
    for i in range(n):
        idx += list(range(i * d + d // 2, (i + 1) * d)) + list(range(i * d, i * d + d // 2))
    return np.array(idx, dtype=np.int32)


def _split_cols(w):
    out, off = {}, 0
    for n, s in zip(IN_NAMES, IN_SIZES):
        out[n] = w[..., off:off + s]
        off += s
    return out


def _pad_cols(a, left, total):
    pads = [(0, 0)] * (a.ndim - 1) + [(left, total - left - a.shape[-1])]
    return jnp.pad(a, pads)


def _tile_last(a, n):
    return jnp.concatenate([a] * n, axis=-1)


def _prepare_weights(w_in, mla_w_uq, mla_w_ukv):
    c = _split_cols(w_in)
    r64_4, r64_1, r32_8, r32_1 = _rot_idx(64, 4), _rot_idx(64, 1), _rot_idx(32, 8), _rot_idx(32, 1)
    parts = {
        "aq": c["a_q"], "ak": c["a_k"], "av": c["a_v"],
        "g": jnp.concatenate([c["a_g"], c["b_g"], c["c_g"], c["d_g"]], -1),
        "bcq": c["b_cq"], "bckv": c["b_ckv"],
        "bkr": _pad_cols(c["b_kr"], 64, 128), "bkr_rot": _pad_cols(c["b_kr"][..., r32_1], 64, 128),
        "cq": c["c_q"], "cq_rot": c["c_q"][..., r64_4],
        "ck": _pad_cols(c["c_k"], 0, 128), "ck_rot": _pad_cols(c["c_k"][..., r64_1], 0, 128),
        "cv": _pad_cols(c["c_v"], 0, 128),
        "ciq": c["c_iq"], "ciq_rot": c["c_iq"][..., r64_4],
        "cik": _pad_cols(c["c_ik"], 0, 128), "cik_rot": _pad_cols(c["c_ik"][..., r64_1], 0, 128),
        "ciw": _pad_cols(c["c_iw"], 0, 128),
        "dq": c["d_q"], "dq_rot": c["d_q"][..., r32_8], "dk": c["d_k"], "dk_rot": c["d_k"][..., r32_8],
        "dv": c["d_v"],
    }
    w = jnp.concatenate([parts[n] for n, _ in _SLABS], axis=-1).astype(BF16)

    depth = w_in.shape[0]
    uq = mla_w_uq.reshape(depth, 256, N_HEADS, MLA_QK)
    uq_x = _pad_cols(uq, 0, 128).reshape(depth, 256, 512)
    uq_r = _pad_cols(uq[..., MLA_NOPE:][..., r32_1], 64, 128).reshape(depth, 256, 512)
    wuq = jnp.concatenate([uq_x, uq_r], -1).astype(BF16)
    ukv = mla_w_ukv.reshape(depth, 128, N_HEADS, 128)
    uk = _pad_cols(ukv[..., :MLA_NOPE], 0, 128).reshape(depth, 128, 512)
    uv = ukv[..., MLA_NOPE:].reshape(depth, 128, 256)
    wukv = jnp.concatenate([uk, uv], -1).astype(BF16)
    return w, wuq, wukv


def _prepare_gains(ln_g, mla_q_norm_g, mla_kv_norm_g, mla_q_g, mla_k_g, dsa_q_g, dsa_k_g, diff_q_g, diff_k_g):
    r32, r64 = _rot_idx(32, 1), _rot_idx(64, 1)

    def row(a):
        return _pad_cols(a, 0, 1024)[:, None, :]

    sb = MLA_QK ** -0.5 * LOG2E
    sc = HEAD_DIM ** -0.5 * LOG2E
    sd = DIFF_DK ** -0.5 * LOG2E
    qb = _tile_last(_pad_cols(mla_q_g, 0, 128), 4) * sb
    qb_rot = _tile_last(_pad_cols(mla_q_g[:, MLA_NOPE:][:, r32], 64, 128), 4) * sb
    kb = _tile_last(_pad_cols(mla_k_g[:, :MLA_NOPE], 0, 128), 4)
    kr = _pad_cols(mla_k_g[:, MLA_NOPE:], 64, 128)
    kr_rot = _pad_cols(mla_k_g[:, MLA_NOPE:][:, r32], 64, 128)
    rows = [ln_g, mla_q_norm_g, mla_kv_norm_g, qb, qb_rot, kb, kr, kr_rot,
            _tile_last(dsa_q_g, 4) * sc, _tile_last(dsa_q_g[:, r64], 4) * sc,
            _tile_last(dsa_k_g, 2), _tile_last(dsa_k_g[:, r64], 2),
            _tile_last(diff_q_g, 8) * sd, _tile_last(diff_q_g[:, r32], 8) * sd,
            _tile_last(diff_k_g, 8), _tile_last(diff_k_g[:, r32], 8)]
    return jnp.concatenate([row(r.astype(F32)) for r in rows], axis=1)


def _rope_tables(seq):
    pos = jnp.arange(seq, dtype=F32)[:, None]

    def cs(d):
        inv = ROPE_THETA ** (-jnp.arange(0, d, 2, dtype=F32) / d)
        ang = pos * inv[None, :]
        cos, sin = jnp.cos(ang), jnp.sin(ang)
        return jnp.concatenate([cos, cos], -1), jnp.concatenate([-sin, sin], -1)

    c32, s32 = cs(32)
    c64, s64 = cs(64)
    one, zero = jnp.ones((seq, 64), F32), jnp.zeros((seq, 64), F32)
    z32 = jnp.zeros((seq, 32), F32)
    tb = jnp.stack([jnp.concatenate([one, c32, z32], -1), jnp.concatenate([zero, s32, z32], -1)])
    t64 = jnp.stack([_tile_last(c64, 2), _tile_last(s64, 2)])
    t32 = jnp.stack([_tile_last(c32, 4), _tile_last(s32, 4)])
    return jnp.stack([tb, t64, t32])


def _segment_mean_matrices():
    lane = np.arange(256)

    def seg(ids, sizes):
        same = ids[:, None] == ids[None, :]
        return np.where(same, 1.0 / sizes[None, :], 0.0)

    in128 = lane % 128
    ids_b = (lane // 128) * 3 + np.where(in128 < 64, 0, np.where(in128 < 96, 1, 2))
    sizes_b = np.where(in128 < 64, 64.0, 32.0)
    mats = [seg(ids_b, sizes_b), seg(lane // 64, np.full(256, 64.0)), seg(lane // 32, np.full(256, 32.0))]
    return jnp.asarray(np.stack(mats), dtype=BF16)


def _dot(a, b):
    return jnp.dot(a, b, preferred_element_type=F32)


def _segment_mean(sq, m):
    hi = sq.astype(BF16)
    lo = (sq - hi.astype(F32)).astype(BF16)
    return _dot(hi, m) + _dot(lo, m)


def _segment_mean_wide(sq, m):
    width = sq.shape[-1]
    if width == LANES:
        return _segment_mean(_tile_last(sq, 2), m)[:, :LANES]
    return jnp.concatenate([_segment_mean(sq[:, 256 * i:256 * (i + 1)], m) for i in range(width // 256)], axis=-1)


def _proj_kernel(x_ref, w_ref, wuq_ref, wukv_ref, gv_ref, seg_ref, tab_ref,
                 qa_ref, ka_ref, va_ref, gate_ref, qb_ref, kb_ref, vb_ref,
                 qc_ref, kc_ref, vc_ref, iq_ref, ik_ref, iw_ref, qd_ref, kd_ref, vd_ref):
    x = x_ref[...]
    r = lax.rsqrt(jnp.mean(x * x, axis=-1, keepdims=True) + EPS)
    h = (x * r * gv_ref[0:1, :]).astype(BF16)

    def proj(name):
        off, width = _SLAB[name]
        return _dot(h, w_ref[:, off:off + width])

    def gain(row, width):
        return gv_ref[row:row + 1, 0:width]

    def put_t(ref, val, rows=None):
        vt = val.T
        ref[0] = (vt if rows is None else vt[:rows]).astype(ref.dtype)

    seg_b, seg_64, seg_32 = seg_ref[0], seg_ref[1], seg_ref[2]
    cos_b, sin_b = tab_ref[0, 0], tab_ref[0, 1]
    cos_64, sin_64 = tab_ref[1, 0], tab_ref[1, 1]
    cos_32, sin_32 = tab_ref[2, 0], tab_ref[2, 1]

    def wide(tab, width):
        return _tile_last(tab, width // LANES) if width > LANES else tab

    def normed_rope(xv, xrot, seg, g_row, grot_row, cos, sin):
        width = xv.shape[-1]
        rr = lax.rsqrt(_segment_mean_wide(xv * xv, seg) + EPS)
        return rr * (xv * (gain(g_row, width) * wide(cos, width)) + xrot * (gain(grot_row, width) * wide(sin, width)))

    put_t(qa_ref, proj("aq") * 0.125)
    ka_ref[...] = proj("ak").astype(BF16)
    put_t(va_ref, proj("av"))

    g = proj("g")
    gate_ref[...] = g * jax.nn.sigmoid(g)

    cq = proj("bcq")
    cq = (cq * lax.rsqrt(jnp.mean(cq * cq, axis=-1, keepdims=True) + EPS) * gain(1, 256)).astype(BF16)
    ckv = proj("bckv")
    ckv = (ckv * lax.rsqrt(jnp.mean(ckv * ckv, axis=-1, keepdims=True) + EPS) * gain(2, 128)).astype(BF16)
    q2 = _dot(cq, wuq_ref[...])
    put_t(qb_ref, normed_rope(q2[:, :512], q2[:, 512:], seg_b, 3, 4, cos_b, sin_b))
    kv2 = _dot(ckv, wukv_ref[...])
    k_nope = kv2[:, :512]
    rk = lax.rsqrt(_segment_mean_wide(k_nope * k_nope, seg_b) + EPS)
    kr_out = normed_rope(proj("bkr"), proj("bkr_rot"), seg_b, 6, 7, cos_b, sin_b)
    kb_ref[...] = (rk * k_nope * gain(5, 512) + _tile_last(kr_out, 4)).astype(BF16)
    put_t(vb_ref, kv2[:, 512:])

    put_t(qc_ref, normed_rope(proj("cq"), proj("cq_rot"), seg_64, 8, 9, cos_64, sin_64))
    kc_ref[...] = normed_rope(proj("ck"), proj("ck_rot"), seg_64, 10, 11, cos_64, sin_64)[:, :HEAD_DIM].astype(BF16)
    put_t(vc_ref, proj("cv"), HEAD_DIM)
    put_t(iq_ref, (proj("ciq") * wide(cos_64, 256) + proj("ciq_rot") * wide(sin_64, 256)) * 0.125)
    ik_ref[...] = (proj("cik") * cos_64 + proj("cik_rot") * sin_64)[:, :HEAD_DIM].astype(BF16)
    put_t(iw_ref, proj("ciw") * 0.5, SUBLANES)

    put_t(qd_ref, normed_rope(proj("dq"), proj("dq_rot"), seg_32, 12, 13, cos_32, sin_32))
    kd_ref[...] = normed_rope(proj("dk"), proj("dk_rot"), seg_32, 14, 15, cos_32, sin_32).astype(BF16)
    put_t(vd_ref, proj("dv"))


_PROJ_OUT = (("qa", 256, BF16, True), ("ka", 256, BF16, False), ("va", 256, BF16, True), ("gate", 1024, F32, False),
             ("qb", 512, BF16, True), ("kb", 512, BF16, False), ("vb", 256, BF16, True),
             ("qc", 256, BF16, True), ("kc", 64, BF16, False), ("vc", 64, BF16, True), ("iq", 256, BF16, True),
             ("ik", 64, BF16, False), ("iw", SUBLANES, F32, True),
             ("qd", 256, BF16, True), ("kd", 256, BF16, False), ("vd", 256, BF16, True))


def _project(x2d, w, wuq, wukv, gv, seg, tabs, seq):
    rows = x2d.shape[0]
    n_tiles = rows // TILE
    n_seq = seq // TILE

    def const(shape):
        return pl.BlockSpec(shape, lambda i: (0,) * len(shape))

    in_specs = [pl.BlockSpec((TILE, D_MODEL), lambda i: (i, 0)),
                const(w.shape), const(wuq.shape), const(wukv.shape), const(gv.shape), const(seg.shape),
                pl.BlockSpec((3, 2, TILE, LANES), lambda i: (0, 0, i % n_seq, 0))]
    out_specs, out_shape = [], []
    for _, width, dt, transposed in _PROJ_OUT:
        if transposed:
            out_specs.append(pl.BlockSpec((1, width, TILE), lambda i: (i, 0, 0)))
            out_shape.append(jax.ShapeDtypeStruct((n_tiles, width, TILE), dt))
        else:
            out_specs.append(pl.BlockSpec((TILE, width), lambda i: (i, 0)))
            out_shape.append(jax.ShapeDtypeStruct((rows, width), dt))
    outs = pl.pallas_call(
        _proj_kernel, grid=(n_tiles,), in_specs=in_specs, out_specs=out_specs, out_shape=out_shape,
        compiler_params=pltpu.CompilerParams(dimension_semantics=("arbitrary",), vmem_limit_bytes=VMEM_LIMIT),
        name="proj",
    )(x2d, w, wuq, wukv, gv, seg, tabs)
    return {n: o for (n, _, _, _), o in zip(_PROJ_OUT, outs)}


def _out_kernel(x_ref, gate_ref, ya_ref, yb_ref, yc_ref, yd_ref, w_ref, o_ref):
    y = jnp.concatenate([ya_ref[...], yb_ref[...], yc_ref[...], yd_ref[...]], axis=-1) * gate_ref[...]
    o_ref[...] = x_ref[...] + _dot(y.astype(BF16), w_ref[...])


def _out_project(x2d, gate, ya, yb, yc, yd, w_out):
    rows = x2d.shape[0]
    row_spec = lambda width: pl.BlockSpec((TILE, width), lambda i: (i, 0))
    return pl.pallas_call(
        _out_kernel, grid=(rows // TILE,),
        in_specs=[row_spec(D_MODEL), row_spec(D_MODEL), row_spec(GROUP), row_spec(GROUP), row_spec(GROUP),
                  row_spec(GROUP), pl.BlockSpec((D_MODEL, D_MODEL), lambda i: (0, 0))],
        out_specs=row_spec(D_MODEL), out_shape=jax.ShapeDtypeStruct((rows, D_MODEL), F32),
        compiler_params=pltpu.CompilerParams(dimension_semantics=("arbitrary",), vmem_limit_bytes=VMEM_LIMIT),
        name="out_proj",
    )(x2d, gate, ya, yb, yc, yd, w_out)


def _tile_iotas():
    return (lax.broadcasted_iota(jnp.int32, (TILE, TILE), 0), lax.broadcasted_iota(jnp.int32, (TILE, TILE), 1))


def _chunk_visible():
    key, query = _tile_iotas()
    return (key >> CHUNK_SHIFT) <= (query >> CHUNK_SHIFT)


def _keep_rows(q, shift, group):
    row = lax.broadcasted_iota(jnp.int32, q.shape, 0)
    return jnp.where((row >> shift) == group, q, jnp.zeros_like(q))


def _online_softmax_step(ss, carry, acc_ref, v_tiles):
    ms, ls = carry
    n = len(ss)
    m_new = [jnp.maximum(ms[h], jnp.max(ss[h], axis=0, keepdims=True)) for h in range(n)]
    alpha = [jnp.exp2(ms[h] - m_new[h]) for h in range(n)]
    ps = [jnp.exp2(ss[h] - m_new[h]) for h in range(n)]
    l_new = [alpha[h] * ls[h] + jnp.sum(ps[h], axis=0, keepdims=True) for h in range(n)]
    pvs = []
    for h in range(n):
        pb = ps[h].astype(BF16)
        if isinstance(v_tiles[h], (list, tuple)):
            pv = _dot(v_tiles[h][0], pb[:TILE])
            for t, v_t in enumerate(v_tiles[h][1:], start=1):
                pv = pv + _dot(v_t, pb[TILE * t:TILE * (t + 1)])
            pvs.append(pv)
        else:
            pvs.append(_dot(v_tiles[h], pb))
    for h in range(n):
        acc_ref[h] = alpha[h] * acc_ref[h] + pvs[h]
    return tuple(m_new), tuple(l_new)


def _softmax_init(n):
    return (tuple(jnp.full((1, TILE), MASKED, F32) for _ in range(n)),
            tuple(jnp.zeros((1, TILE), F32) for _ in range(n)))


def _attn_call(kernel, q_arrays, k_arrays, v_arrays, extra, batch, seq, scratch, name):
    nq = seq // TILE
    in_specs = ([pl.BlockSpec((1,) + a.shape[1:], lambda b, i: (b * nq + i, 0, 0)) for a in q_arrays]
                + [pl.BlockSpec((seq, a.shape[1]), lambda b, i: (b, 0)) for a in k_arrays]
                + [pl.BlockSpec((nq,) + a.shape[1:], lambda b, i: (b, 0, 0)) for a in v_arrays]
                + [pl.BlockSpec(a.shape, lambda b, i: (0,) * a.ndim) for a in extra])
    return pl.pallas_call(
        kernel, grid=(batch, nq), in_specs=in_specs,
        out_specs=pl.BlockSpec((TILE, GROUP), lambda b, i: (b * nq + i, 0)),
        out_shape=jax.ShapeDtypeStruct((batch * seq, GROUP), F32),
        scratch_shapes=scratch,
        compiler_params=pltpu.CompilerParams(dimension_semantics=("arbitrary", "arbitrary"),
                                             vmem_limit_bytes=VMEM_LIMIT),
        name=name,
    )(*q_arrays, *k_arrays, *v_arrays, *extra)


def _store_heads(o_ref, heads):
    o_ref[...] = jnp.concatenate(heads, axis=0).T


def _attn_a_kernel(q_ref, k_ref, v_ref, o_ref, acc_ref):
    i = pl.program_id(1)
    key, query = _tile_iotas()
    strict = key < query
    later = (query > key).astype(BF16)
    qs = [_keep_rows(q_ref[0, LANES * (h // 2):LANES * (h // 2 + 1), :], 6, h % 2) for h in range(N_HEADS)]
    acc_ref[...] = jnp.zeros_like(acc_ref)

    def tile(j, rs, on_diagonal):
        off = pl.multiple_of(j * TILE, TILE)
        heads = range(N_HEADS)
        zs = [_dot(k_ref[pl.ds(off, TILE), LANES * (h // 2):LANES * (h // 2 + 1)], qs[h]) for h in heads]
        sps = [jnp.maximum(z, 0.0) + jnp.log1p(jnp.exp(-jnp.abs(z))) for z in zs]
        lms = [-sp for sp in sps]
        if on_diagonal:
            lms = [jnp.where(strict, lm, 0.0) for lm in lms]
        cums = []
        for lm in lms:
            l1 = lm.astype(BF16)
            r1 = lm - l1.astype(F32)
            l2 = r1.astype(BF16)
            l3 = (r1 - l2.astype(F32)).astype(BF16)
            cums.append(_dot(later, l1) + _dot(later, l2) + _dot(later, l3))
        a_s = [jnp.exp(zs[h] - sps[h] + (rs[h] + cums[h])) for h in heads]
        if on_diagonal:
            a_s = [jnp.where(strict, a, 0.0) for a in a_s]
        pvs = [_dot(v_ref[j, HEAD_DIM * h:HEAD_DIM * (h + 1), :], a_s[h].astype(BF16)) for h in heads]
        for h in heads:
            acc_ref[h] += pvs[h]
        return tuple(rs[h] + jnp.sum(lms[h], axis=0, keepdims=True) for h in heads)

    def r_max(rs):
        return jnp.max(jnp.maximum(jnp.maximum(rs[0], rs[1]), jnp.maximum(rs[2], rs[3])))

    rs0 = tile(i, tuple(jnp.zeros((1, TILE), F32) for _ in range(N_HEADS)), True)

    def cond(state):
        j, _, rmax = state
        return jnp.logical_and(j >= 0, rmax > EXP_UNDERFLOW)

    def body(state):
        j, rs, _ = state
        rs = tile(j, rs, False)
        return j - 1, rs, r_max(rs)

    lax.while_loop(cond, body, (i - 1, rs0, r_max(rs0)))
    _store_heads(o_ref, [acc_ref[h] for h in range(N_HEADS)])


def _attn_b_kernel(q_ref, k_ref, v_ref, o_ref, acc_ref):
    i = pl.program_id(1)
    visible = _chunk_visible()
    qs = [q_ref[0, LANES * h:LANES * (h + 1), :] for h in range(N_HEADS)]
    acc_ref[...] = jnp.zeros_like(acc_ref)

    def tile(j, carry, on_diagonal):
        off = pl.multiple_of(j * TILE, TILE)
        ss = [_dot(k_ref[pl.ds(off, TILE), LANES * h:LANES * (h + 1)], qs[h]) for h in range(N_HEADS)]
        if on_diagonal:
            ss = [jnp.where(visible, s, MASKED) for s in ss]
        v_tiles = [v_ref[j, HEAD_DIM * h:HEAD_DIM * (h + 1), :] for h in range(N_HEADS)]
        return _online_softmax_step(ss, carry, acc_ref, v_tiles)

    def tile_pair(jj, carry):
        off = pl.multiple_of(jj * (2 * TILE), 2 * TILE)
        ss = [_dot(k_ref[pl.ds(off, 2 * TILE), LANES * h:LANES * (h + 1)], qs[h]) for h in range(N_HEADS)]
        v_tiles = [[v_ref[2 * jj + u, HEAD_DIM * h:HEAD_DIM * (h + 1), :] for u in range(2)] for h in range(N_HEADS)]
        return _online_softmax_step(ss, carry, acc_ref, v_tiles)

    carry = lax.fori_loop(0, i // 2, tile_pair, _softmax_init(N_HEADS))
    carry = lax.cond(i % 2 == 1, lambda c: tile(i - 1, c, False), lambda c: c, carry)
    _, ls = tile(i, carry, True)
    _store_heads(o_ref, [acc_ref[h] / ls[h] for h in range(N_HEADS)])


def _attn_d_kernel(q_ref, k_ref, v_ref, par_ref, gain_ref, o_ref, acc_ref):
    i = pl.program_id(1)
    visible = _chunk_visible()
    n_sub = 2 * N_HEADS
    qs = [_keep_rows(q_ref[0, LANES * (s // 4):LANES * (s // 4 + 1), :], 5, s % 4) for s in range(n_sub)]
    acc_ref[...] = jnp.zeros_like(acc_ref)

    def tile(j, carry, on_diagonal):
        off = pl.multiple_of(j * TILE, TILE)
        ss = [_dot(k_ref[pl.ds(off, TILE), LANES * (s_i // 4):LANES * (s_i // 4 + 1)], qs[s_i])
              for s_i in range(n_sub)]
        if on_diagonal:
            ss = [jnp.where(visible, s, MASKED) for s in ss]
        v_tiles = [v_ref[j, HEAD_DIM * (s_i // 2):HEAD_DIM * (s_i // 2 + 1), :] for s_i in range(n_sub)]
        return _online_softmax_step(ss, carry, acc_ref, v_tiles)

    def tile_pair(jj, carry):
        off = pl.multiple_of(jj * (2 * TILE), 2 * TILE)
        ss = [_dot(k_ref[pl.ds(off, 2 * TILE), LANES * (s_i // 4):LANES * (s_i // 4 + 1)], qs[s_i])
              for s_i in range(n_sub)]
        v_tiles = [[v_ref[2 * jj + u, HEAD_DIM * (s_i // 2):HEAD_DIM * (s_i // 2 + 1), :] for u in range(2)]
                   for s_i in range(n_sub)]
        return _online_softmax_step(ss, carry, acc_ref, v_tiles)

    carry = lax.fori_loop(0, i // 2, tile_pair, _softmax_init(n_sub))
    carry = lax.cond(i % 2 == 1, lambda c: tile(i - 1, c, False), lambda c: c, carry)
    _, ls = tile(i, carry, True)

    lam = (jnp.exp(jnp.sum(par_ref[0:1, :] * par_ref[1:2, :], axis=-1, keepdims=True))
           - jnp.exp(jnp.sum(par_ref[2:3, :] * par_ref[3:4, :], axis=-1, keepdims=True)) + par_ref[4:5, 0:1])
    ys = []
    for h in range(N_HEADS):
        o = acc_ref[2 * h] / ls[2 * h] - lam * (acc_ref[2 * h + 1] / ls[2 * h + 1])
        ms_o = jnp.mean(o * o, axis=0, keepdims=True)
        ys.append(o * lax.rsqrt(ms_o + EPS) * gain_ref[...])
    _store_heads(o_ref, ys)


def _ordered_key(score):
    bits = lax.bitcast_convert_type(score, jnp.int32)
    return jnp.where(bits < 0, jnp.int32(INT_MIN) - bits, bits)


def _attn_c_kernel(q_ref, iq_ref, iw_ref, k_ref, ik_ref, v_ref, o_ref, keys_ref, acc_ref, *, topk):
    i = pl.program_id(1)
    n_kv = i + 1
    n_pairs = (n_kv + 1) // 2
    visible = _chunk_visible()
    kf = float(topk)

    iqs = [iq_ref[0, HEAD_DIM * h:HEAD_DIM * (h + 1), :] for h in range(N_HEADS)]
    iws = [iw_ref[0, h:h + 1, :] for h in range(N_HEADS)]

    def score_tile(j, on_diagonal):
        ik_t = ik_ref[pl.ds(pl.multiple_of(j * TILE, TILE), TILE), :]
        logits = [_dot(ik_t, iqs[h]) for h in range(N_HEADS)]
        score = iws[0] * jnp.maximum(logits[0], 0.0)
        for h in range(1, N_HEADS):
            score = score + iws[h] * jnp.maximum(logits[h], 0.0)
        key = _ordered_key(score)
        if on_diagonal:
            key = jnp.where(visible, key, jnp.int32(KEY_NEG_INF))
        keys_ref[j] = key
        return jnp.max(key.reshape(TILE // SUBLANES, SUBLANES, TILE), axis=0)

    top = lax.fori_loop(0, i, lambda j, m: jnp.maximum(m, score_tile(j, False)),
                        jnp.full((SUBLANES, TILE), INT_MIN, jnp.int32))
    top = jnp.max(jnp.maximum(top, score_tile(i, True)), axis=0, keepdims=True)

    @pl.when(n_kv % 2 == 1)
    def _():
        keys_ref[n_kv] = jnp.full((TILE, TILE), INT_MIN, jnp.int32)

    def count_at_least(probe):
        def body(jj, acc):
            for u in range(2):
                ind = jnp.where(keys_ref[2 * jj + u] >= probe, 1.0, 0.0)
                acc = acc + jnp.sum(ind.reshape(TILE // SUBLANES, SUBLANES, TILE), axis=0)
            return acc
        acc = lax.fori_loop(0, n_pairs, body, jnp.zeros((SUBLANES, TILE), F32))
        return jnp.sum(acc, axis=0, keepdims=True)

    def open_count(lo, hi):
        return jnp.sum(jnp.where(hi > lo + 1, 1.0, 0.0))

    def search_step(lo, hi, c_lo, c_hi, gallop):
        is_open = hi > lo + 1
        probe = (lo >> 1) + (hi >> 1) + (lo & hi & 1)
        probe = jnp.where(jnp.logical_and(lo < 0, hi > 0), 0, probe)
        probe = jnp.where(jnp.logical_and(lo == 0, hi > 1), 1, probe)
        if gallop is not None:
            down_from_top = top + 1 - gallop
            use = jnp.logical_and(jnp.logical_and(lo == INT_MIN, down_from_top > 1), down_from_top < hi)
            probe = jnp.where(use, down_from_top, probe)
        c = count_at_least(probe)
        up = jnp.logical_and(c >= kf, is_open)
        down = jnp.logical_and(c < kf, is_open)
        lo = jnp.where(up, probe, lo)
        c_lo = jnp.where(up, c, c_lo)
        hi = jnp.where(down, probe, hi)
        c_hi = jnp.where(down, c, c_hi)
        exact = jnp.logical_and(c == kf, is_open)
        hi = jnp.where(exact, lo + 1, hi)
        c_hi = jnp.where(exact, 0.0, c_hi)
        return lo, hi, c_lo, c_hi

    def search_body(state):
        lo, hi, c_lo, c_hi = search_step(*search_step(*state[:4], None), None)
        return lo, hi, c_lo, c_hi, open_count(lo, hi)

    lo0 = jnp.full((1, TILE), INT_MIN, jnp.int32)
    n_keys = (n_kv * TILE).astype(F32)
    st = (lo0, top + 1, jnp.zeros((1, TILE), F32) + n_keys, jnp.zeros((1, TILE), F32))
    for g in range(GALLOP_STEPS):
        st = search_step(*st, BINADE << g)
    thr, _, n_ge, n_gt, _ = lax.while_loop(lambda s_: s_[4] > 0.0, search_body, st + (open_count(st[0], st[1]),))

    need = kf - n_gt
    surplus = jnp.logical_and(n_ge - n_gt > need, thr > KEY_NEG_INF)
    any_surplus = jnp.max(jnp.where(surplus, 1.0, 0.0)) > 0.0

    @pl.when(any_surplus)
    def _():
        key, query = _tile_iotas()
        upto = (query <= key).astype(BF16)

        def body(j, seen):
            kt = keys_ref[j]
            tie = jnp.where(kt == thr, 1.0, 0.0)
            rank = seen + _dot(upto, tie.astype(BF16))
            keys_ref[j] = jnp.where(jnp.logical_and(kt == thr, rank > need), jnp.int32(INT_MIN), kt)
            return seen + jnp.sum(tie, axis=0, keepdims=True)

        lax.fori_loop(0, n_kv, body, jnp.zeros((1, TILE), F32))

    thr_sel = jnp.maximum(thr, jnp.int32(KEY_NEG_INF + 1))
    qs = [q_ref[0, HEAD_DIM * h:HEAD_DIM * (h + 1), :] for h in range(N_HEADS)]
    acc_ref[...] = jnp.zeros_like(acc_ref)

    def attn_body(jj, carry):
        k_t = k_ref[pl.ds(pl.multiple_of(jj * (2 * TILE), 2 * TILE), 2 * TILE), :]
        ss = [_dot(k_t, qs[h]) for h in range(N_HEADS)]
        sel = keys_ref[pl.ds(2 * jj, 2)].reshape(2 * TILE, TILE) >= thr_sel
        ss = [jnp.where(sel, s, MASKED) for s in ss]
        return _online_softmax_step(ss, carry, acc_ref, [[v_ref[2 * jj], v_ref[2 * jj + 1]]] * N_HEADS)

    _, ls = lax.fori_loop(0, n_pairs, attn_body, _softmax_init(N_HEADS))
    _store_heads(o_ref, [acc_ref[h] / ls[h] for h in range(N_HEADS)])


def _diff_params(lq1, lk1, lq2, lk2, subln_g, lam_init):
    par = jnp.concatenate([
        _pad_cols(jnp.stack([lq1, lk1, lq2, lk2]), 0, LANES),
        jnp.full((1, LANES), lam_init, F32),
        jnp.zeros((3, LANES), F32)], axis=0).astype(F32)
    gain = jnp.broadcast_to((subln_g * (1.0 - lam_init)).astype(F32)[:, None], (HEAD_DIM, TILE))
    return par, gain


def _layer_groups(x2d, batch, seq, w, wuq, wukv, gv, seg, tabs, dpar, topk):
    p = _project(x2d, w, wuq, wukv, gv, seg, tabs, seq)
    head_acc = pltpu.VMEM((N_HEADS, HEAD_DIM, TILE), F32)
    ya = _attn_call(_attn_a_kernel, [p["qa"]], [p["ka"]], [p["va"]], [], batch, seq, [head_acc], "attn_a")
    yb = _attn_call(_attn_b_kernel, [p["qb"]], [p["kb"]], [p["vb"]], [], batch, seq, [head_acc], "attn_b")
    yc = _attn_call(functools.partial(_attn_c_kernel, topk=topk), [p["qc"], p["iq"], p["iw"]],
                    [p["kc"], p["ik"]], [p["vc"]], [], batch, seq,
                    [pltpu.VMEM((seq // TILE + 1, TILE, TILE), jnp.int32), head_acc], "attn_c")
    yd = _attn_call(_attn_d_kernel, [p["qd"]], [p["kd"]], [p["vd"]], list(dpar), batch, seq,
                    [pltpu.VMEM((2 * N_HEADS, HEAD_DIM, TILE), F32)], "attn_d")
    return p, ya, yb, yc, yd


def _layer(x2d, batch, seq, w, wuq, wukv, gv, seg, tabs, dpar, w_out, topk):
    p, ya, yb, yc, yd = _layer_groups(x2d, batch, seq, w, wuq, wukv, gv, seg, tabs, dpar, topk)
    return _out_project(x2d, p["gate"], ya, yb, yc, yd, w_out)


def kernel(x, ln_g, w_in, mla_q_norm_g, mla_kv_norm_g, mla_w_uq, mla_w_ukv, mla_q_g, mla_k_g, dsa_q_g, dsa_k_g,
           diff_q_g, diff_k_g, diff_lq1, diff_lk1, diff_lq2, diff_lk2, diff_subln_g, w_out):
    batch, seq, _ = x.shape
    depth = w_in.shape[0]
    assert seq % (2 * TILE) == 0
    topk = min(DSA_TOPK_MAX, seq // 4)

    w, wuq, wukv = _prepare_weights(w_in, mla_w_uq, mla_w_ukv)
    gv = _prepare_gains(ln_g, mla_q_norm_g, mla_kv_norm_g, mla_q_g, mla_k_g, dsa_q_g, dsa_k_g, diff_q_g, diff_k_g)
    seg = _segment_mean_matrices()
    tabs = _rope_tables(seq)
    w_out_bf = w_out.astype(BF16)

    x2d = x.reshape(batch * seq, D_MODEL)
    for layer in range(depth):
        lam_init = 0.8 - 0.6 * math.exp(-0.3 * layer)
        dpar = _diff_params(diff_lq1[layer], diff_lk1[layer], diff_lq2[layer], diff_lk2[layer],
                            diff_subln_g[layer], lam_init)
        x2d = _layer(x2d, batch, seq, w[layer], wuq[layer], wukv[layer], gv[layer], seg, tabs, dpar,
                     w_out_bf[layer], topk)
    return x2d.reshape(batch, seq, D_MODEL)
```

```python
import functools
import math

import numpy as np
import jax
import jax.numpy as jnp
from jax import lax
from jax.experimental import pallas as pl
from jax.experimental.pallas import tpu as pltpu

F32 = jnp.float32
BF16 = jnp.bfloat16

D_MODEL = 1024
GROUP = 256
HEAD_DIM = 64
N_HEADS = 4
CHUNK_SHIFT = 6
ROPE_THETA = 10000.0
EPS = 1e-6
MLA_NOPE, MLA_ROPE, MLA_QK = 64, 32, 96
DSA_TOPK_MAX = 256
DIFF_DK = 32
LOG2E = 1.4426950408889634

IN_SIZES = (256, 256, 256, 256, 256, 128, 32, 256, 256, 64, 64, 256, 256, 64, 4, 256, 256, 256, 256)
IN_NAMES = ("a_q", "a_k", "a_v", "a_g", "b_cq", "b_ckv", "b_kr", "b_g", "c_q", "c_k", "c_v", "c_g",
            "c_iq", "c_ik", "c_iw", "d_q", "d_k", "d_v", "d_g")

LANES = 128
SUBLANES = 8
TILE = 256
VMEM_LIMIT = 56 * 1024 * 1024
MASKED = -1e30
INT_MIN = -2147483648
KEY_NEG_INF = -2139095040
BINADE = 1 << 23
GALLOP_STEPS = 3
SCORE_BOUND_LIMIT = 40.0
EXP_UNDERFLOW = -105.0

_SLABS = (("aq", 256), ("ak", 256), ("av", 256), ("g", 1024), ("bcq", 256), ("bckv", 128),
          ("bkr", 128), ("bkr_rot", 128), ("cq", 256), ("cq_rot", 256), ("ck", 128), ("ck_rot", 128),
          ("cv", 128), ("ciq", 256), ("ciq_rot", 256), ("cik", 128), ("cik_rot", 128), ("ciw", 128),
          ("dq", 256), ("dq_rot", 256), ("dk", 256), ("dk_rot", 256), ("dv", 256))
_SLAB = {}
_off = 0
for _n, _w in _SLABS:
    _SLAB[_n] = (_off, _w)
    _off += _w
W_COLS = _off


def _rot_idx(d, n):
    idx = []
    for i in range(n):
        idx += list(range(i * d + d // 2, (i + 1) * d)) + list(range(i * d, i * d + d // 2))
    return np.array(idx, dtype=np.int32)


def _split_cols(w):
    out, off = {}, 0
    for n, s in zip(IN_NAMES, IN_SIZES):
        out[n] = w[..., off:off + s]
        off += s
    return out


def _pad_cols(a, left, total):
    pads = [(0, 0)] * (a.ndim - 1) + [(left, total - left - a.shape[-1])]
    return jnp.pad(a, pads)


def _tile_last(a, n):
    return jnp.concatenate([a] * n, axis=-1)


def _prepare_weights(w_in, mla_w_uq, mla_w_ukv):
    c = _split_cols(w_in)
    r64_4, r64_1, r32_8, r32_1 = _rot_idx(64, 4), _rot_idx(64, 1), _rot_idx(32, 8), _rot_idx(32, 1)
    parts = {
        "aq": c["a_q"], "ak": c["a_k"], "av": c["a_v"],
        "g": jnp.concatenate([c["a_g"], c["b_g"], c["c_g"], c["d_g"]], -1),
        "bcq": c["b_cq"], "bckv": c["b_ckv"],
        "bkr": _pad_cols(c["b_kr"], 64, 128), "bkr_rot": _pad_cols(c["b_kr"][..., r32_1], 64, 128),
        "cq": c["c_q"], "cq_rot": c["c_q"][..., r64_4],
        "ck": _pad_cols(c["c_k"], 0, 128), "ck_rot": _pad_cols(c["c_k"][..., r64_1], 0, 128),
        "cv": _pad_cols(c["c_v"], 0, 128),
        "ciq": c["c_iq"], "ciq_rot": c["c_iq"][..., r64_4],
        "cik": _pad_cols(c["c_ik"], 0, 128), "cik_rot": _pad_cols(c["c_ik"][..., r64_1], 0, 128),
        "ciw": _pad_cols(c["c_iw"], 0, 128),
        "dq": c["d_q"], "dq_rot": c["d_q"][..., r32_8], "dk": c["d_k"], "dk_rot": c["d_k"][..., r32_8],
        "dv": c["d_v"],
    }
    w = jnp.concatenate([parts[n] for n, _ in _SLABS], axis=-1).astype(BF16)

    depth = w_in.shape[0]
    uq = mla_w_uq.reshape(depth, 256, N_HEADS, MLA_QK)
    uq_x = _pad_cols(uq, 0, 128).reshape(depth, 256, 512)
    uq_r = _pad_cols(uq[..., MLA_NOPE:][..., r32_1], 64, 128).reshape(depth, 256, 512)
    wuq = jnp.concatenate([uq_x, uq_r], -1).astype(BF16)
    ukv = mla_w_ukv.reshape(depth, 128, N_HEADS, 128)
    uk = _pad_cols(ukv[..., :MLA_NOPE], 0, 128).reshape(depth, 128, 512)
    uv = ukv[..., MLA_NOPE:].reshape(depth, 128, 256)
    wukv = jnp.concatenate([uk, uv], -1).astype(BF16)
    return w, wuq, wukv


def _prepare_gains(ln_g, mla_q_norm_g, mla_kv_norm_g, mla_q_g, mla_k_g, dsa_q_g, dsa_k_g, diff_q_g, diff_k_g):
    r32, r64 = _rot_idx(32, 1), _rot_idx(64, 1)

    def row(a):
        return _pad_cols(a, 0, 1024)[:, None, :]

    sb = MLA_QK ** -0.5 * LOG2E
    sc = HEAD_DIM ** -0.5 * LOG2E
    sd = DIFF_DK ** -0.5 * LOG2E
    qb = _tile_last(_pad_cols(mla_q_g, 0, 128), 4) * sb
    qb_rot = _tile_last(_pad_cols(mla_q_g[:, MLA_NOPE:][:, r32], 64, 128), 4) * sb
    kb = _tile_last(_pad_cols(mla_k_g[:, :MLA_NOPE], 0, 128), 4)
    kr = _pad_cols(mla_k_g[:, MLA_NOPE:], 64, 128)
    kr_rot = _pad_cols(mla_k_g[:, MLA_NOPE:][:, r32], 64, 128)
    rows = [ln_g, mla_q_norm_g, mla_kv_norm_g, qb, qb_rot, kb, kr, kr_rot,
            _tile_last(dsa_q_g, 4) * sc, _tile_last(dsa_q_g[:, r64], 4) * sc,
            _tile_last(dsa_k_g, 2), _tile_last(dsa_k_g[:, r64], 2),
            _tile_last(diff_q_g, 8) * sd, _tile_last(diff_q_g[:, r32], 8) * sd,
            _tile_last(diff_k_g, 8), _tile_last(diff_k_g[:, r32], 8)]
    return jnp.concatenate([row(r.astype(F32)) for r in rows], axis=1)


def _rope_tables(seq):
    pos = jnp.arange(seq, dtype=F32)[:, None]

    def cs(d):
        inv = ROPE_THETA ** (-jnp.arange(0, d, 2, dtype=F32) / d)
        ang = pos * inv[None, :]
        cos, sin = jnp.cos(ang), jnp.sin(ang)
        return jnp.concatenate([cos, cos], -1), jnp.concatenate([-sin, sin], -1)

    c32, s32 = cs(32)
    c64, s64 = cs(64)
    one, zero = jnp.ones((seq, 64), F32), jnp.zeros((seq, 64), F32)
    z32 = jnp.zeros((seq, 32), F32)
    tb = jnp.stack([jnp.concatenate([one, c32, z32], -1), jnp.concatenate([zero, s32, z32], -1)])
    t64 = jnp.stack([_tile_last(c64, 2), _tile_last(s64, 2)])
    t32 = jnp.stack([_tile_last(c32, 4), _tile_last(s32, 4)])
    return jnp.stack([tb, t64, t32])


def _segment_mean_matrices():
    lane = np.arange(256)

    def seg(ids, sizes):
        same = ids[:, None] == ids[None, :]
        return np.where(same, 1.0 / sizes[None, :], 0.0)

    in128 = lane % 128
    ids_b = (lane // 128) * 3 + np.where(in128 < 64, 0, np.where(in128 < 96, 1, 2))
    sizes_b = np.where(in128 < 64, 64.0, 32.0)
    mats = [seg(ids_b, sizes_b), seg(lane // 64, np.full(256, 64.0)), seg(lane // 32, np.full(256, 32.0))]
    return jnp.asarray(np.stack(mats), dtype=BF16)


def _dot(a, b):
    return jnp.dot(a, b, preferred_element_type=F32)


def _segment_mean(sq, m):
    hi = sq.astype(BF16)
    lo = (sq - hi.astype(F32)).astype(BF16)
    return _dot(hi, m) + _dot(lo, m)


def _segment_mean_wide(sq, m):
    width = sq.shape[-1]
    if width == LANES:
        return _segment_mean(_tile_last(sq, 2), m)[:, :LANES]
    return jnp.concatenate([_segment_mean(sq[:, 256 * i:256 * (i + 1)], m) for i in range(width // 256)], axis=-1)


def _proj_kernel(x_ref, w_ref, wuq_ref, wukv_ref, gv_ref, seg_ref, tab_ref,
                 qa_ref, ka_ref, va_ref, gate_ref, qb_ref, kb_ref, vb_ref,
                 qc_ref, kc_ref, vc_ref, iq_ref, ik_ref, iw_ref, qd_ref, kd_ref, vd_ref):
    x = x_ref[...]
    r = lax.rsqrt(jnp.mean(x * x, axis=-1, keepdims=True) + EPS)
    h = (x * r * gv_ref[0:1, :]).astype(BF16)

    def proj(name):
        off, width = _SLAB[name]
        return _dot(h, w_ref[:, off:off + width])

    def gain(row, width):
        return gv_ref[row:row + 1, 0:width]

    def put_t(ref, val, rows=None):
        vt = val.T
        ref[0] = (vt if rows is None else vt[:rows]).astype(ref.dtype)

    seg_b, seg_64, seg_32 = seg_ref[0], seg_ref[1], seg_ref[2]
    cos_b, sin_b = tab_ref[0, 0], tab_ref[0, 1]
    cos_64, sin_64 = tab_ref[1, 0], tab_ref[1, 1]
    cos_32, sin_32 = tab_ref[2, 0], tab_ref[2, 1]

    def wide(tab, width):
        return _tile_last(tab, width // LANES) if width > LANES else tab

    def normed_rope(xv, xrot, seg, g_row, grot_row, cos, sin):
        width = xv.shape[-1]
        rr = lax.rsqrt(_segment_mean_wide(xv * xv, seg) + EPS)
        return rr * (xv * (gain(g_row, width) * wide(cos, width)) + xrot * (gain(grot_row, width) * wide(sin, width)))

    put_t(qa_ref, proj("aq") * 0.125)
    ka_ref[...] = proj("ak").astype(BF16)
    put_t(va_ref, proj("av"))

    g = proj("g")
    gate_ref[...] = g * jax.nn.sigmoid(g)

    cq = proj("bcq")
    cq = (cq * lax.rsqrt(jnp.mean(cq * cq, axis=-1, keepdims=True) + EPS) * gain(1, 256)).astype(BF16)
    ckv = proj("bckv")
    ckv = (ckv * lax.rsqrt(jnp.mean(ckv * ckv, axis=-1, keepdims=True) + EPS) * gain(2, 128)).astype(BF16)
    q2 = _dot(cq, wuq_ref[...])
    put_t(qb_ref, normed_rope(q2[:, :512], q2[:, 512:], seg_b, 3, 4, cos_b, sin_b))
    kv2 = _dot(ckv, wukv_ref[...])
    k_nope = kv2[:, :512]
    rk = lax.rsqrt(_segment_mean_wide(k_nope * k_nope, seg_b) + EPS)
    kr_out = normed_rope(proj("bkr"), proj("bkr_rot"), seg_b, 6, 7, cos_b, sin_b)
    kb_ref[...] = (rk * k_nope * gain(5, 512) + _tile_last(kr_out, 4)).astype(BF16)
    put_t(vb_ref, kv2[:, 512:])

    put_t(qc_ref, normed_rope(proj("cq"), proj("cq_rot"), seg_64, 8, 9, cos_64, sin_64))
    kc_ref[...] = normed_rope(proj("ck"), proj("ck_rot"), seg_64, 10, 11, cos_64, sin_64)[:, :HEAD_DIM].astype(BF16)
    put_t(vc_ref, proj("cv"), HEAD_DIM)
    put_t(iq_ref, (proj("ciq") * wide(cos_64, 256) + proj("ciq_rot") * wide(sin_64, 256)) * 0.125)
    ik_ref[...] = (proj("cik") * cos_64 + proj("cik_rot") * sin_64)[:, :HEAD_DIM].astype(BF16)
    put_t(iw_ref, proj("ciw") * 0.5, SUBLANES)

    put_t(qd_ref, normed_rope(proj("dq"), proj("dq_rot"), seg_32, 12, 13, cos_32, sin_32))
    kd_ref[...] = normed_rope(proj("dk"), proj("dk_rot"), seg_32, 14, 15, cos_32, sin_32).astype(BF16)
    put_t(vd_ref, proj("dv"))


_PROJ_OUT = (("qa", 256, BF16, True), ("ka", 256, BF16, False), ("va", 256, BF16, True), ("gate", 1024, F32, False),
             ("qb", 512, BF16, True), ("kb", 512, BF16, False), ("vb", 256, BF16, True),
             ("qc", 256, BF16, True), ("kc", 64, BF16, False), ("vc", 64, BF16, True), ("iq", 256, BF16, True),
             ("ik", 64, BF16, False), ("iw", SUBLANES, F32, True),
             ("qd", 256, BF16, True), ("kd", 256, BF16, False), ("vd", 256, BF16, True))


def _project(x2d, w, wuq, wukv, gv, seg, tabs, seq):
    rows = x2d.shape[0]
    n_tiles = rows // TILE
    n_seq = seq // TILE

    def const(shape):
        return pl.BlockSpec(shape, lambda i: (0,) * len(shape))

    in_specs = [pl.BlockSpec((TILE, D_MODEL), lambda i: (i, 0)),
                const(w.shape), const(wuq.shape), const(wukv.shape), const(gv.shape), const(seg.shape),
                pl.BlockSpec((3, 2, TILE, LANES), lambda i: (0, 0, i % n_seq, 0))]
    out_specs, out_shape = [], []
    for _, width, dt, transposed in _PROJ_OUT:
        if transposed:
            out_specs.append(pl.BlockSpec((1, width, TILE), lambda i: (i, 0, 0)))
            out_shape.append(jax.ShapeDtypeStruct((n_tiles, width, TILE), dt))
        else:
            out_specs.append(pl.BlockSpec((TILE, width), lambda i: (i, 0)))
            out_shape.append(jax.ShapeDtypeStruct((rows, width), dt))
    outs = pl.pallas_call(
        _proj_kernel, grid=(n_tiles,), in_specs=in_specs, out_specs=out_specs, out_shape=out_shape,
        compiler_params=pltpu.CompilerParams(dimension_semantics=("arbitrary",), vmem_limit_bytes=VMEM_LIMIT),
        name="proj",
    )(x2d, w, wuq, wukv, gv, seg, tabs)
    return {n: o for (n, _, _, _), o in zip(_PROJ_OUT, outs)}


def _out_kernel(x_ref, gate_ref, ya_ref, yb_ref, yc_ref, yd_ref, w_ref, o_ref):
    y = jnp.concatenate([ya_ref[...], yb_ref[...], yc_ref[...], yd_ref[...]], axis=-1) * gate_ref[...]
    o_ref[...] = x_ref[...] + _dot(y.astype(BF16), w_ref[...])


def _out_project(x2d, gate, ya, yb, yc, yd, w_out):
    rows = x2d.shape[0]
    row_spec = lambda width: pl.BlockSpec((TILE, width), lambda i: (i, 0))
    return pl.pallas_call(
        _out_kernel, grid=(rows // TILE,),
        in_specs=[row_spec(D_MODEL), row_spec(D_MODEL), row_spec(GROUP), row_spec(GROUP), row_spec(GROUP),
                  row_spec(GROUP), pl.BlockSpec((D_MODEL, D_MODEL), lambda i: (0, 0))],
        out_specs=row_spec(D_MODEL), out_shape=jax.ShapeDtypeStruct((rows, D_MODEL), F32),
        compiler_params=pltpu.CompilerParams(dimension_semantics=("arbitrary",), vmem_limit_bytes=VMEM_LIMIT),
        name="out_proj",
    )(x2d, gate, ya, yb, yc, yd, w_out)


def _tile_iotas():
    return (lax.broadcasted_iota(jnp.int32, (TILE, TILE), 0), lax.broadcasted_iota(jnp.int32, (TILE, TILE), 1))


def _chunk_visible():
    key, query = _tile_iotas()
    return (key >> CHUNK_SHIFT) <= (query >> CHUNK_SHIFT)


def _keep_rows(q, shift, group):
    row = lax.broadcasted_iota(jnp.int32, q.shape, 0)
    return jnp.where((row >> shift) == group, q, jnp.zeros_like(q))


def _softmax_step(ss, carry, acc_ref, v_tiles, bounded):
    ms, ls = carry
    n = len(ss)
    if bounded:
        m_new = list(ms)
        ps = [jnp.exp2(s) for s in ss]
        l_new = [ls[h] + jnp.sum(ps[h], axis=0, keepdims=True) for h in range(n)]
    else:
        m_new = [jnp.maximum(ms[h], jnp.max(ss[h], axis=0, keepdims=True)) for h in range(n)]
        alpha = [jnp.exp2(ms[h] - m_new[h]) for h in range(n)]
        ps = [jnp.exp2(ss[h] - m_new[h]) for h in range(n)]
        l_new = [alpha[h] * ls[h] + jnp.sum(ps[h], axis=0, keepdims=True) for h in range(n)]
    pvs = []
    for h in range(n):
        pb = ps[h].astype(BF16)
        if isinstance(v_tiles[h], (list, tuple)):
            pv = _dot(v_tiles[h][0], pb[:TILE])
            for t, v_t in enumerate(v_tiles[h][1:], start=1):
                pv = pv + _dot(v_t, pb[TILE * t:TILE * (t + 1)])
            pvs.append(pv)
        else:
            pvs.append(_dot(v_tiles[h], pb))
    for h in range(n):
        acc_ref[h] = (acc_ref[h] if bounded else alpha[h] * acc_ref[h]) + pvs[h]
    return tuple(m_new), tuple(l_new)


def _scores_bounded(bound_ref):
    return jnp.max(bound_ref[0:1, :]) <= SCORE_BOUND_LIMIT


def _softmax_init(n):
    return (tuple(jnp.full((1, TILE), MASKED, F32) for _ in range(n)),
            tuple(jnp.zeros((1, TILE), F32) for _ in range(n)))


def _attn_call(kernel, q_arrays, k_arrays, v_arrays, extra, batch, seq, scratch, name):
    nq = seq // TILE
    in_specs = ([pl.BlockSpec((1,) + a.shape[1:], lambda b, i: (b * nq + i, 0, 0)) for a in q_arrays]
                + [pl.BlockSpec((seq, a.shape[1]), lambda b, i: (b, 0)) for a in k_arrays]
                + [pl.BlockSpec((nq,) + a.shape[1:], lambda b, i: (b, 0, 0)) for a in v_arrays]
                + [pl.BlockSpec(a.shape, lambda b, i: (0,) * a.ndim) for a in extra])
    return pl.pallas_call(
        kernel, grid=(batch, nq), in_specs=in_specs,
        out_specs=pl.BlockSpec((TILE, GROUP), lambda b, i: (b * nq + i, 0)),
        out_shape=jax.ShapeDtypeStruct((batch * seq, GROUP), F32),
        scratch_shapes=scratch,
        compiler_params=pltpu.CompilerParams(dimension_semantics=("arbitrary", "arbitrary"),
                                             vmem_limit_bytes=VMEM_LIMIT),
        name=name,
    )(*q_arrays, *k_arrays, *v_arrays, *extra)


def _store_heads(o_ref, heads):
    o_ref[...] = jnp.concatenate(heads, axis=0).T


def _attn_a_kernel(q_ref, k_ref, v_ref, o_ref, acc_ref):
    i = pl.program_id(1)
    key, query = _tile_iotas()
    strict = key < query
    later = (query > key).astype(BF16)
    qs = [_keep_rows(q_ref[0, LANES * (h // 2):LANES * (h // 2 + 1), :], 6, h % 2) for h in range(N_HEADS)]
    acc_ref[...] = jnp.zeros_like(acc_ref)

    def tile(j, rs, on_diagonal):
        off = pl.multiple_of(j * TILE, TILE)
        heads = range(N_HEADS)
        zs = [_dot(k_ref[pl.ds(off, TILE), LANES * (h // 2):LANES * (h // 2 + 1)], qs[h]) for h in heads]
        sps = [jnp.maximum(z, 0.0) + jnp.log1p(jnp.exp(-jnp.abs(z))) for z in zs]
        lms = [-sp for sp in sps]
        if on_diagonal:
            lms = [jnp.where(strict, lm, 0.0) for lm in lms]
        cums = []
        for lm in lms:
            l1 = lm.astype(BF16)
            r1 = lm - l1.astype(F32)
            l2 = r1.astype(BF16)
            l3 = (r1 - l2.astype(F32)).astype(BF16)
            cums.append(_dot(later, l1) + _dot(later, l2) + _dot(later, l3))
        a_s = [jnp.exp(zs[h] - sps[h] + (rs[h] + cums[h])) for h in heads]
        if on_diagonal:
            a_s = [jnp.where(strict, a, 0.0) for a in a_s]
        pvs = [_dot(v_ref[j, HEAD_DIM * h:HEAD_DIM * (h + 1), :], a_s[h].astype(BF16)) for h in heads]
        for h in heads:
            acc_ref[h] += pvs[h]
        return tuple(rs[h] + jnp.sum(lms[h], axis=0, keepdims=True) for h in heads)

    def r_max(rs):
        return jnp.max(jnp.maximum(jnp.maximum(rs[0], rs[1]), jnp.maximum(rs[2], rs[3])))

    rs0 = tile(i, tuple(jnp.zeros((1, TILE), F32) for _ in range(N_HEADS)), True)

    def cond(state):
        j, _, rmax = state
        return jnp.logical_and(j >= 0, rmax > EXP_UNDERFLOW)

    def body(state):
        j, rs, _ = state
        rs = tile(j, rs, False)
        return j - 1, rs, r_max(rs)

    lax.while_loop(cond, body, (i - 1, rs0, r_max(rs0)))
    _store_heads(o_ref, [acc_ref[h] for h in range(N_HEADS)])


def _attn_b_kernel(q_ref, k_ref, v_ref, bound_ref, o_ref, acc_ref):
    i = pl.program_id(1)
    visible = _chunk_visible()
    qs = [q_ref[0, LANES * h:LANES * (h + 1), :] for h in range(N_HEADS)]
    acc_ref[...] = jnp.zeros_like(acc_ref)

    def sweep(bounded):
        def tile(j, carry, on_diagonal):
            off = pl.multiple_of(j * TILE, TILE)
            ss = [_dot(k_ref[pl.ds(off, TILE), LANES * h:LANES * (h + 1)], qs[h]) for h in range(N_HEADS)]
            if on_diagonal:
                ss = [jnp.where(visible, s, MASKED) for s in ss]
            v_tiles = [v_ref[j, HEAD_DIM * h:HEAD_DIM * (h + 1), :] for h in range(N_HEADS)]
            return _softmax_step(ss, carry, acc_ref, v_tiles, bounded)

        def tile_pair(jj, carry):
            off = pl.multiple_of(jj * (2 * TILE), 2 * TILE)
            ss = [_dot(k_ref[pl.ds(off, 2 * TILE), LANES * h:LANES * (h + 1)], qs[h]) for h in range(N_HEADS)]
            v_tiles = [[v_ref[2 * jj + u, HEAD_DIM * h:HEAD_DIM * (h + 1), :] for u in range(2)]
                       for h in range(N_HEADS)]
            return _softmax_step(ss, carry, acc_ref, v_tiles, bounded)

        carry = lax.fori_loop(0, i // 2, tile_pair, _softmax_init(N_HEADS))
        carry = lax.cond(i % 2 == 1, lambda c: tile(i - 1, c, False), lambda c: c, carry)
        return tile(i, carry, True)[1]

    ls = lax.cond(_scores_bounded(bound_ref), lambda: sweep(True), lambda: sweep(False))
    _store_heads(o_ref, [acc_ref[h] / ls[h] for h in range(N_HEADS)])


def _attn_d_kernel(q_ref, k_ref, v_ref, par_ref, gain_ref, o_ref, acc_ref):
    i = pl.program_id(1)
    visible = _chunk_visible()
    n_sub = 2 * N_HEADS
    qs = [_keep_rows(q_ref[0, LANES * (s // 4):LANES * (s // 4 + 1), :], 5, s % 4) for s in range(n_sub)]
    acc_ref[...] = jnp.zeros_like(acc_ref)

    def sweep(bounded):
        def tile(j, carry, on_diagonal):
            off = pl.multiple_of(j * TILE, TILE)
            ss = [_dot(k_ref[pl.ds(off, TILE), LANES * (s_i // 4):LANES * (s_i // 4 + 1)], qs[s_i])
                  for s_i in range(n_sub)]
            if on_diagonal:
                ss = [jnp.where(visible, s, MASKED) for s in ss]
            v_tiles = [v_ref[j, HEAD_DIM * (s_i // 2):HEAD_DIM * (s_i // 2 + 1), :] for s_i in range(n_sub)]
            return _softmax_step(ss, carry, acc_ref, v_tiles, bounded)

        def tile_pair(jj, carry):
            off = pl.multiple_of(jj * (2 * TILE), 2 * TILE)
            ss = [_dot(k_ref[pl.ds(off, 2 * TILE), LANES * (s_i // 4):LANES * (s_i // 4 + 1)], qs[s_i])
                  for s_i in range(n_sub)]
            v_tiles = [[v_ref[2 * jj + u, HEAD_DIM * (s_i // 2):HEAD_DIM * (s_i // 2 + 1), :] for u in range(2)]
                       for s_i in range(n_sub)]
            return _softmax_step(ss, carry, acc_ref, v_tiles, bounded)

        carry = lax.fori_loop(0, i // 2, tile_pair, _softmax_init(n_sub))
        carry = lax.cond(i % 2 == 1, lambda c: tile(i - 1, c, False), lambda c: c, carry)
        return tile(i, carry, True)[1]

    ls = lax.cond(_scores_bounded(par_ref.at[5:6]), lambda: sweep(True), lambda: sweep(False))

    lam = (jnp.exp(jnp.sum(par_ref[0:1, :] * par_ref[1:2, :], axis=-1, keepdims=True))
           - jnp.exp(jnp.sum(par_ref[2:3, :] * par_ref[3:4, :], axis=-1, keepdims=True)) + par_ref[4:5, 0:1])
    ys = []
    for h in range(N_HEADS):
        o = acc_ref[2 * h] / ls[2 * h] - lam * (acc_ref[2 * h + 1] / ls[2 * h + 1])
        ms_o = jnp.mean(o * o, axis=0, keepdims=True)
        ys.append(o * lax.rsqrt(ms_o + EPS) * gain_ref[...])
    _store_heads(o_ref, ys)


def _ordered_key(score):
    bits = lax.bitcast_convert_type(score, jnp.int32)
    return jnp.where(bits < 0, jnp.int32(INT_MIN) - bits, bits)


def _attn_c_kernel(q_ref, iq_ref, iw_ref, k_ref, ik_ref, v_ref, bound_ref, o_ref, keys_ref, acc_ref, *, topk):
    i = pl.program_id(1)
    n_kv = i + 1
    n_pairs = (n_kv + 1) // 2
    visible = _chunk_visible()
    kf = float(topk)

    iqs = [iq_ref[0, HEAD_DIM * h:HEAD_DIM * (h + 1), :] for h in range(N_HEADS)]
    iws = [iw_ref[0, h:h + 1, :] for h in range(N_HEADS)]

    def score_tiles(j, n, on_diagonal):
        ik_t = ik_ref[pl.ds(pl.multiple_of(j * TILE, TILE), n * TILE), :]
        logits = [_dot(ik_t, iqs[h]) for h in range(N_HEADS)]
        score = iws[0] * jnp.maximum(logits[0], 0.0)
        for h in range(1, N_HEADS):
            score = score + iws[h] * jnp.maximum(logits[h], 0.0)
        key = _ordered_key(score)
        if on_diagonal:
            key = jnp.where(visible, key, jnp.int32(KEY_NEG_INF))
        keys_ref[pl.ds(j, n)] = key.reshape(n, TILE, TILE)
        return jnp.max(key.reshape(n * TILE // SUBLANES, SUBLANES, TILE), axis=0)

    top = lax.fori_loop(0, i // 2, lambda jj, m: jnp.maximum(m, score_tiles(2 * jj, 2, False)),
                        jnp.full((SUBLANES, TILE), INT_MIN, jnp.int32))
    top = lax.cond(i % 2 == 1, lambda m: jnp.maximum(m, score_tiles(i - 1, 1, False)), lambda m: m, top)
    top = jnp.max(jnp.maximum(top, score_tiles(i, 1, True)), axis=0, keepdims=True)

    @pl.when(n_kv % 2 == 1)
    def _():
        keys_ref[n_kv] = jnp.full((TILE, TILE), INT_MIN, jnp.int32)

    def count_at_least(probe):
        def body(jj, acc):
            for u in range(2):
                ind = jnp.where(keys_ref[2 * jj + u] >= probe, 1.0, 0.0)
                acc = acc + jnp.sum(ind.reshape(TILE // SUBLANES, SUBLANES, TILE), axis=0)
            return acc
        acc = lax.fori_loop(0, n_pairs, body, jnp.zeros((SUBLANES, TILE), F32))
        return jnp.sum(acc, axis=0, keepdims=True)

    def open_count(lo, hi):
        return jnp.sum(jnp.where(hi > lo + 1, 1.0, 0.0))

    def search_step(lo, hi, c_lo, c_hi, gallop):
        is_open = hi > lo + 1
        probe = (lo >> 1) + (hi >> 1) + (lo & hi & 1)
        probe = jnp.where(jnp.logical_and(lo < 0, hi > 0), 0, probe)
        probe = jnp.where(jnp.logical_and(lo == 0, hi > 1), 1, probe)
        if gallop is not None:
            down_from_top = top + 1 - gallop
            use = jnp.logical_and(jnp.logical_and(lo == INT_MIN, down_from_top > 1), down_from_top < hi)
            probe = jnp.where(use, down_from_top, probe)
        c = count_at_least(probe)
        up = jnp.logical_and(c >= kf, is_open)
        down = jnp.logical_and(c < kf, is_open)
        lo = jnp.where(up, probe, lo)
        c_lo = jnp.where(up, c, c_lo)
        hi = jnp.where(down, probe, hi)
        c_hi = jnp.where(down, c, c_hi)
        exact = jnp.logical_and(c == kf, is_open)
        hi = jnp.where(exact, lo + 1, hi)
        c_hi = jnp.where(exact, 0.0, c_hi)
        return lo, hi, c_lo, c_hi

    def search_body(state):
        lo, hi, c_lo, c_hi = search_step(*search_step(*state[:4], None), None)
        return lo, hi, c_lo, c_hi, open_count(lo, hi)

    lo0 = jnp.full((1, TILE), INT_MIN, jnp.int32)
    n_keys = (n_kv * TILE).astype(F32)
    st = (lo0, top + 1, jnp.zeros((1, TILE), F32) + n_keys, jnp.zeros((1, TILE), F32))
    for g in range(GALLOP_STEPS):
        st = search_step(*st, BINADE << g)
    thr, _, n_ge, n_gt, _ = lax.while_loop(lambda s_: s_[4] > 0.0, search_body, st + (open_count(st[0], st[1]),))

    need = kf - n_gt
    surplus = jnp.logical_and(n_ge - n_gt > need, thr > KEY_NEG_INF)
    any_surplus = jnp.max(jnp.where(surplus, 1.0, 0.0)) > 0.0

    @pl.when(any_surplus)
    def _():
        key, query = _tile_iotas()
        upto = (query <= key).astype(BF16)

        def body(j, seen):
            kt = keys_ref[j]
            tie = jnp.where(kt == thr, 1.0, 0.0)
            rank = seen + _dot(upto, tie.astype(BF16))
            keys_ref[j] = jnp.where(jnp.logical_and(kt == thr, rank > need), jnp.int32(INT_MIN), kt)
            return seen + jnp.sum(tie, axis=0, keepdims=True)

        lax.fori_loop(0, n_kv, body, jnp.zeros((1, TILE), F32))

    thr_sel = jnp.maximum(thr, jnp.int32(KEY_NEG_INF + 1))
    qs = [q_ref[0, HEAD_DIM * h:HEAD_DIM * (h + 1), :] for h in range(N_HEADS)]
    acc_ref[...] = jnp.zeros_like(acc_ref)

    def sweep(bounded):
        def attn_body(jj, carry):
            k_t = k_ref[pl.ds(pl.multiple_of(jj * (2 * TILE), 2 * TILE), 2 * TILE), :]
            ss = [_dot(k_t, qs[h]) for h in range(N_HEADS)]
            sel = keys_ref[pl.ds(2 * jj, 2)].reshape(2 * TILE, TILE) >= thr_sel
            ss = [jnp.where(sel, s, MASKED) for s in ss]
            return _softmax_step(ss, carry, acc_ref, [[v_ref[2 * jj], v_ref[2 * jj + 1]]] * N_HEADS, bounded)

        return lax.fori_loop(0, n_pairs, attn_body, _softmax_init(N_HEADS))[1]

    ls = lax.cond(_scores_bounded(bound_ref), lambda: sweep(True), lambda: sweep(False))
    _store_heads(o_ref, [acc_ref[h] / ls[h] for h in range(N_HEADS)])


def _score_bounds(mla_q_g, mla_k_g, dsa_q_g, dsa_k_g, diff_q_g, diff_k_g):
    def part(g, n):
        return n * jnp.max(g * g, axis=-1)

    b = jnp.sqrt((part(mla_q_g[:, :MLA_NOPE], 64) + part(mla_q_g[:, MLA_NOPE:], 32))
                 * (part(mla_k_g[:, :MLA_NOPE], 64) + part(mla_k_g[:, MLA_NOPE:], 32))) * (MLA_QK ** -0.5 * LOG2E)
    c = jnp.sqrt(part(dsa_q_g, 64) * part(dsa_k_g, 64)) * (HEAD_DIM ** -0.5 * LOG2E)
    d = jnp.sqrt(part(diff_q_g, 32) * part(diff_k_g, 32)) * (DIFF_DK ** -0.5 * LOG2E)
    return b.astype(F32), c.astype(F32), d.astype(F32)


def _bound_rows(bound):
    return jnp.zeros((SUBLANES, LANES), F32) + bound


def _diff_params(lq1, lk1, lq2, lk2, subln_g, lam_init, bound):
    par = jnp.concatenate([
        _pad_cols(jnp.stack([lq1, lk1, lq2, lk2]), 0, LANES),
        jnp.full((1, LANES), lam_init, F32),
        jnp.zeros((1, LANES), F32) + bound,
        jnp.zeros((2, LANES), F32)], axis=0).astype(F32)
    gain = jnp.broadcast_to((subln_g * (1.0 - lam_init)).astype(F32)[:, None], (HEAD_DIM, TILE))
    return par, gain


def _layer_groups(x2d, batch, seq, w, wuq, wukv, gv, seg, tabs, dpar, bounds, topk):
    p = _project(x2d, w, wuq, wukv, gv, seg, tabs, seq)
    head_acc = pltpu.VMEM((N_HEADS, HEAD_DIM, TILE), F32)
    ya = _attn_call(_attn_a_kernel, [p["qa"]], [p["ka"]], [p["va"]], [], batch, seq, [head_acc], "attn_a")
    yb = _attn_call(_attn_b_kernel, [p["qb"]], [p["kb"]], [p["vb"]], [_bound_rows(bounds[0])], batch, seq,
                    [head_acc], "attn_b")
    yc = _attn_call(functools.partial(_attn_c_kernel, topk=topk), [p["qc"], p["iq"], p["iw"]],
                    [p["kc"], p["ik"]], [p["vc"]], [_bound_rows(bounds[1])], batch, seq,
                    [pltpu.VMEM((seq // TILE + 1, TILE, TILE), jnp.int32), head_acc], "attn_c")
    yd = _attn_call(_attn_d_kernel, [p["qd"]], [p["kd"]], [p["vd"]], list(dpar), batch, seq,
                    [pltpu.VMEM((2 * N_HEADS, HEAD_DIM, TILE), F32)], "attn_d")
    return p, ya, yb, yc, yd


def _layer(x2d, batch, seq, w, wuq, wukv, gv, seg, tabs, dpar, bounds, w_out, topk):
    p, ya, yb, yc, yd = _layer_groups(x2d, batch, seq, w, wuq, wukv, gv, seg, tabs, dpar, bounds, topk)
    return _out_project(x2d, p["gate"], ya, yb, yc, yd, w_out)


def kernel(x, ln_g, w_in, mla_q_norm_g, mla_kv_norm_g, mla_w_uq, mla_w_ukv, mla_q_g, mla_k_g, dsa_q_g, dsa_k_g,
           diff_q_g, diff_k_g, diff_lq1, diff_lk1, diff_lq2, diff_lk2, diff_subln_g, w_out):
    batch, seq, _ = x.shape
    depth = w_in.shape[0]
    assert seq % (2 * TILE) == 0
    topk = min(DSA_TOPK_MAX, seq // 4)

    w, wuq, wukv = _prepare_weights(w_in, mla_w_uq, mla_w_ukv)
    gv = _prepare_gains(ln_g, mla_q_norm_g, mla_kv_norm_g, mla_q_g, mla_k_g, dsa_q_g, dsa_k_g, diff_q_g, diff_k_g)
    seg = _segment_mean_matrices()
    tabs = _rope_tables(seq)
    w_out_bf = w_out.astype(BF16)
    bound_b, bound_c, bound_d = _score_bounds(mla_q_g, mla_k_g, dsa_q_g, dsa_k_g, diff_q_g, diff_k_g)

    x2d = x.reshape(batch * seq, D_MODEL)
    for layer in range(depth):
        lam_init = 0.8 - 0.6 * math.exp(-0.3 * layer)
        dpar = _diff_params(diff_lq1[layer], diff_lk1[layer], diff_lq2[layer], diff_lk2[layer],
                            diff_subln_g[layer], lam_init, bound_d[layer])
        x2d = _layer(x2d, batch, seq, w[layer], wuq[layer], wukv[layer], gv[layer], seg, tabs, dpar,
                     (bound_b[layer], bound_c[layer]), w_out_bf[layer], topk)
    return x2d.reshape(batch, seq, D_MODEL)
```
